```python
import jax
import jax.numpy as jnp
from jax import lax
import numpy as np

D_MODEL = 1024
BATCH = 4
SEQ = 8192
DEPTH = 1
DEC_BATCH = 1
DEC_SEQ = 16384
PAST_LEN = 128

D_A = D_MODEL
D_B = D_MODEL
D_MIX = D_A + D_B
H_A = 8
HD_A = D_A // H_A
H_B = 8
HD_B = D_B // H_B
GMLP_CHUNK = 128
HGRN_CHUNK = 64
D_IN = 3 * D_A + 5 * D_B
EPS = 1e-6

kernel_name = 'bidir_gmlp_hgrn2_hybrid'


def rmsnorm(x, g):
    xf = x.astype(jnp.float32)
    y = xf * lax.rsqrt(jnp.mean(xf * xf, axis=-1, keepdims=True) + EPS)
    return (y * g.astype(jnp.float32)).astype(x.dtype)


def layernorm(x, g, b):
    xf = x.astype(jnp.float32)
    xc = xf - jnp.mean(xf, axis=-1, keepdims=True)
    y = xc * lax.rsqrt(jnp.mean(xc * xc, axis=-1, keepdims=True) + EPS)
    return (y * g.astype(jnp.float32) + b.astype(jnp.float32)).astype(x.dtype)


def gmlp_spatial_gating(u, v, ln_g, ln_b, w_s, b_s):
    bsz, seq, _ = u.shape
    vn = layernorm(v, ln_g, ln_b).reshape(bsz, seq // GMLP_CHUNK, GMLP_CHUNK, H_A, HD_A)
    s = jnp.einsum('hts,bnshc->bnthc', w_s, vn) + b_s.T[:, :, None]
    return u * s.reshape(bsz, seq, D_A)


def gla_chunk_scan(q, k, g, v):
    bsz, seq, nh, dk = q.shape
    dv = v.shape[-1]
    n_chunks = seq // HGRN_CHUNK

    def to_chunks(t):
        return t.reshape(bsz, n_chunks, HGRN_CHUNK, nh, t.shape[-1]).transpose(1, 0, 3, 2, 4)

    incl = jnp.tril(jnp.ones((HGRN_CHUNK, HGRN_CHUNK), dtype=bool))[:, :, None]

    def step(state, chunk):
        qc, kc, gc, vc = chunk
        b = jnp.cumsum(gc, axis=2)
        o_inter = jnp.einsum('bhtk,bhkv->bhtv', qc * jnp.exp(b), state)
        diff = b[:, :, :, None, :] - b[:, :, None, :, :]
        decay = jnp.exp(jnp.where(incl, diff, -jnp.inf))
        scores = jnp.einsum('bhtsk,bhsk->bhts', qc[:, :, :, None, :] * decay, kc)
        o = o_inter + jnp.einsum('bhts,bhsv->bhtv', scores, vc)
        b_end = b[:, :, -1:, :]
        new_state = (jnp.exp(b_end[:, :, 0, :])[..., None] * state
                     + jnp.einsum('bhsk,bhsv->bhkv', kc * jnp.exp(b_end - b), vc))
        return new_state, o

    s0 = jnp.zeros((bsz, nh, dk, dv), jnp.float32)
    _, o = lax.scan(step, s0, (to_chunks(q), to_chunks(k), to_chunks(g), to_chunks(v)))
    return o.transpose(1, 0, 3, 2, 4).reshape(bsz, seq, nh, dv)


def hgrn2_bidirectional(q, f_fwd, f_bwd, i, lb_fwd, lb_bwd, gn_g, z):
    bsz, seq, _ = q.shape

    def heads(t):
        return t.astype(jnp.float32).reshape(bsz, seq, H_B, HD_B)

    qh = heads(jax.nn.silu(q))
    ih = heads(i)

    def one_direction(f_logit, lb, reverse):
        fg = lb + (1.0 - lb) * jax.nn.sigmoid(f_logit.astype(jnp.float32))
        args = (qh, heads(1.0 - fg), heads(jnp.log(fg)), ih)
        if reverse:
            args = tuple(jnp.flip(t, axis=1) for t in args)
        o = gla_chunk_scan(*args)
        return jnp.flip(o, axis=1) if reverse else o

    o = one_direction(f_fwd, lb_fwd, False) + one_direction(f_bwd, lb_bwd, True)
    o = o * lax.rsqrt(jnp.mean(o * o, axis=-1, keepdims=True) + EPS)
    o = o.reshape(bsz, seq, D_B) * gn_g.astype(jnp.float32)
    return (o * jax.nn.silu(z.astype(jnp.float32))).astype(z.dtype)


def encoder_trunk(x, norm_g, w_in, ln_v_g, ln_v_b, w_s, b_s, lb_params, gn_g, w_out, final_g):
    lower_bounds = jnp.cumsum(jax.nn.softmax(lb_params.astype(jnp.float32), axis=1), axis=1)
    cuts = [D_A, 2 * D_A, 3 * D_A, 3 * D_A + D_B, 3 * D_A + 2 * D_B,
            3 * D_A + 3 * D_B, 3 * D_A + 4 * D_B]
    for layer in range(DEPTH):
        h = rmsnorm(x, norm_g[layer])
        proj = jnp.einsum('bld,de->ble', h, w_in[layer])
        u_a, v_a, z_a, q_b, f_fwd, f_bwd, i_b, z_b = jnp.split(proj, cuts, axis=-1)
        out_a = gmlp_spatial_gating(u_a, v_a, ln_v_g[layer], ln_v_b[layer],
                                    w_s[layer], b_s[layer]) * jax.nn.silu(z_a)
        out_b = hgrn2_bidirectional(q_b, f_fwd, f_bwd, i_b, lower_bounds[0, layer],
                                    lower_bounds[1, layer], gn_g[layer], z_b)
        mixed = jnp.concatenate([out_a, out_b], axis=-1)
        x = x + jnp.einsum('ble,ed->bld', mixed, w_out[layer]).astype(x.dtype)
    return rmsnorm(x, final_g)


def setup_inputs(seed: int = 0) -> dict:
    key = jax.random.key(seed)
    ks = jax.random.split(key, 12)
    f32 = jnp.float32

    def nrm(k, shape, scale):
        return scale * jax.random.normal(k, shape, f32)

    return {
        'x_prompt': jax.random.normal(ks[0], (BATCH, SEQ, D_MODEL), f32),
        'x_sample': jax.random.normal(ks[1], (DEC_BATCH, DEC_SEQ, D_MODEL), f32),
        'norm_g': 1.0 + nrm(ks[2], (DEPTH, D_MODEL), 0.02),
        'w_in': nrm(ks[3], (DEPTH, D_MODEL, D_IN), D_MODEL ** -0.5),
        'ln_v_g': 1.0 + nrm(ks[4], (DEPTH, D_A), 0.02),
        'ln_v_b': nrm(ks[5], (DEPTH, D_A), 0.02),
        'w_s': nrm(ks[6], (DEPTH, H_A, GMLP_CHUNK, GMLP_CHUNK), GMLP_CHUNK ** -0.5),
        'b_s': 1.0 + nrm(ks[7], (DEPTH, H_A, GMLP_CHUNK), 0.02),
        'lb_params': nrm(ks[8], (2, DEPTH + 1, D_B), 0.1),
        'gn_g': 1.0 + nrm(ks[9], (DEPTH, D_B), 0.02),
        'w_out': nrm(ks[10], (DEPTH, D_MIX, D_MODEL), D_MIX ** -0.5),
        'final_g': 1.0 + nrm(ks[11], (D_MODEL,), 0.02),
    }


def reference(x_prompt, x_sample, norm_g, w_in, ln_v_g, ln_v_b, w_s, b_s, lb_params, gn_g, w_out, final_g):
    y_prompt = encoder_trunk(x_prompt, norm_g, w_in, ln_v_g, ln_v_b, w_s, b_s,
                             lb_params, gn_g, w_out, final_g)
    y_sample = encoder_trunk(x_sample, norm_g, w_in, ln_v_g, ln_v_b, w_s, b_s,
                             lb_params, gn_g, w_out, final_g)
    return (y_prompt, y_sample)
```

```python
import functools

import jax
import jax.numpy as jnp
from jax import lax
from jax.experimental import pallas as pl
from jax.experimental.pallas import tpu as pltpu

D_MODEL = 1024
N_HEADS = 8
HEAD_DIM = 128
GMLP_CHUNK = 128
CHUNK = 64
EPS = 1e-6
TILE = 256
VMEM_LIMIT_BYTES = 56 * 1024 * 1024

_BF16 = jnp.bfloat16
_F32 = jnp.float32
_NT = (((1,), (1,)), ((), ()))
_TN = (((0,), (0,)), ((), ()))


def _sigmoid(x):
    return 1.0 / (1.0 + jnp.exp(-x))


def _rmsnorm_rows(x, gain):
    ms = jnp.mean(x * x, axis=-1, keepdims=True)
    return (x * lax.rsqrt(ms + EPS)) * gain


def _lower_bound(lb_pair):
    m = jnp.max(lb_pair, axis=0, keepdims=True)
    e = jnp.exp(lb_pair - m)
    return e[0:1, :] / jnp.sum(e, axis=0, keepdims=True)


def _gates(f_logit, lb):
    fg = lb + (1.0 - lb) * _sigmoid(f_logit)
    return jnp.log(fg), 1.0 - fg


def _cumsum_rows(g, tri3):
    hi = g.astype(_BF16)
    r1 = g - hi.astype(_F32)
    mid = r1.astype(_BF16)
    lo = (r1 - mid.astype(_F32)).astype(_BF16)
    g3 = jnp.concatenate([hi, mid, lo], axis=0)
    return jnp.dot(tri3, g3, preferred_element_type=_F32)


def _gla_chunk(q, k, g, v, st_ref, tri3, mask, end_row, emit):
    b = _cumsum_rows(g, tri3)
    total = b[end_row:end_row + 1, :]
    half = 0.5 * total
    d = b - half
    e_half = jnp.exp(half)
    qt = q * jnp.exp(d)
    kt = k * jnp.exp(-d)
    qi = (qt * e_half).astype(_BF16)
    ks = (kt * e_half).astype(_BF16)
    qt = qt.astype(_BF16)
    kt = kt.astype(_BF16)
    vb = v.astype(_BF16)
    decay = e_half * e_half
    for h in range(N_HEADS):
        sl = slice(h * HEAD_DIM, (h + 1) * HEAD_DIM)
        scores = lax.dot_general(qt[:, sl], kt[:, sl], _NT, preferred_element_type=_F32)
        scores = jnp.where(mask, scores, 0.0).astype(_BF16)
        st = st_ref[h]
        o = jnp.dot(scores, vb[:, sl], preferred_element_type=_F32)
        o = o + lax.dot_general(qi[:, sl], st.astype(_BF16), _NT, preferred_element_type=_F32)
        upd = lax.dot_general(vb[:, sl], ks[:, sl], _TN, preferred_element_type=_F32)
        st_ref[h] = st * decay[:, sl] + upd
        emit(h, o)


def _bwd_kernel(x_ref, ng_ref, w_ref, lb_ref, tri3_ref, mask_ref, o_ref,
                q_s, k_s, g_s, v_s, st_ref):
    @pl.when(pl.program_id(1) == 0)
    def _():
        st_ref[...] = jnp.zeros_like(st_ref)

    x = x_ref[0]
    hb = _rmsnorm_rows(x, ng_ref[...]).astype(_BF16)
    proj = jnp.dot(hb, w_ref[...], preferred_element_type=_F32)
    qp = proj[:, 0:D_MODEL]
    q_s[...] = qp * _sigmoid(qp)
    lb = _lower_bound(lb_ref[...])
    g, k = _gates(proj[:, D_MODEL:2 * D_MODEL], lb)
    g_s[...] = g
    k_s[...] = k
    v_s[...] = proj[:, 2 * D_MODEL:3 * D_MODEL]

    tri3 = tri3_ref[...]
    mask = mask_ref[...] > 0.5
    n_chunks = x.shape[0] // CHUNK

    def body(j, carry):
        r0 = pl.multiple_of((n_chunks - 1 - j) * CHUNK, CHUNK)
        rows = pl.ds(r0, CHUNK)

        def emit(h, o):
            o_ref[0, rows, h * HEAD_DIM:(h + 1) * HEAD_DIM] = o

        _gla_chunk(q_s[rows, :], k_s[rows, :], g_s[rows, :], v_s[rows, :],
                   st_ref, tri3, mask, 0, emit)
        return carry

    lax.fori_loop(0, n_chunks, body, 0)


def _main_kernel(x_ref, ob_ref, ng_ref, wa_ref, wb_ref, lng_ref, lnb_ref, ws_ref, bs_ref,
                 lb_ref, gn_ref, wo_ref, fg_ref, tri3_ref, mask_ref, y_ref,
                 q_s, k_s, g_s, v_s, z_s, mix_s, st_ref):
    @pl.when(pl.program_id(1) == 0)
    def _():
        st_ref[...] = jnp.zeros_like(st_ref)

    x = x_ref[0]
    tile = x.shape[0]
    hb = _rmsnorm_rows(x, ng_ref[...]).astype(_BF16)

    pa = jnp.dot(hb, wa_ref[...], preferred_element_type=_F32)
    vv = pa[:, D_MODEL:2 * D_MODEL]
    vc = vv - jnp.mean(vv, axis=-1, keepdims=True)
    vn = vc * lax.rsqrt(jnp.mean(vc * vc, axis=-1, keepdims=True) + EPS)
    vn = (vn * lng_ref[...] + lnb_ref[...]).astype(_BF16)
    za = pa[:, 2 * D_MODEL:3 * D_MODEL]
    gate_a = pa[:, 0:D_MODEL] * (za * _sigmoid(za))
    for n in range(tile // GMLP_CHUNK):
        rs = slice(n * GMLP_CHUNK, (n + 1) * GMLP_CHUNK)
        for h in range(N_HEADS):
            sl = slice(h * HEAD_DIM, (h + 1) * HEAD_DIM)
            s = jnp.dot(ws_ref[h], vn[rs, sl], preferred_element_type=_F32) + bs_ref[:, sl]
            mix_s[rs, sl] = (gate_a[rs, sl] * s).astype(_BF16)

    pb = jnp.dot(hb, wb_ref[...], preferred_element_type=_F32)
    qp = pb[:, 0:D_MODEL]
    q_s[...] = qp * _sigmoid(qp)
    lb = _lower_bound(lb_ref[...])
    g, k = _gates(pb[:, D_MODEL:2 * D_MODEL], lb)
    g_s[...] = g
    k_s[...] = k
    v_s[...] = pb[:, 2 * D_MODEL:3 * D_MODEL]
    zb = pb[:, 3 * D_MODEL:4 * D_MODEL]
    z_s[...] = (zb * _sigmoid(zb)) * gn_ref[...]

    tri3 = tri3_ref[...]
    mask = mask_ref[...] > 0.5
    n_chunks = tile // CHUNK

    def body(j, carry):
        r0 = pl.multiple_of(j * CHUNK, CHUNK)
        rows = pl.ds(r0, CHUNK)

        def emit(h, o):
            sl = slice(h * HEAD_DIM, (h + 1) * HEAD_DIM)
            o = o + ob_ref[0, rows, sl]
            o = o * lax.rsqrt(jnp.mean(o * o, axis=-1, keepdims=True) + EPS)
            mix_s[rows, D_MODEL + h * HEAD_DIM:D_MODEL + (h + 1) * HEAD_DIM] = (
                o * z_s[rows, sl]).astype(_BF16)

        _gla_chunk(q_s[rows, :], k_s[rows, :], g_s[rows, :], v_s[rows, :],
                   st_ref, tri3, mask, CHUNK - 1, emit)
        return carry

    lax.fori_loop(0, n_chunks, body, 0)

    out = jnp.dot(mix_s[...], wo_ref[...], preferred_element_type=_F32)
    y_ref[0] = _rmsnorm_rows(x + out, fg_ref[...])


def _const_spec(shape):
    return pl.BlockSpec(shape, lambda b, j: (0,) * len(shape), pipeline_mode=pl.Buffered(1))


def _direction_constants(reverse):
    r = lax.broadcasted_iota(jnp.int32, (CHUNK, CHUNK), 0)
    c = lax.broadcasted_iota(jnp.int32, (CHUNK, CHUNK), 1)
    keep = (c >= r) if reverse else (c <= r)
    mask = keep.astype(_F32)
    tri3 = jnp.concatenate([mask, mask, mask], axis=1).astype(_BF16)
    return tri3, mask


def _trunk(x, p, tile=TILE):
    bsz, seq, d = x.shape
    assert d == D_MODEL and seq % tile == 0 and tile % GMLP_CHUNK == 0
    nt = seq // tile
    cparams = pltpu.CompilerParams(dimension_semantics=("arbitrary", "arbitrary"),
                                   vmem_limit_bytes=VMEM_LIMIT_BYTES)
    act_f32 = pltpu.VMEM((tile, D_MODEL), _F32)
    state = pltpu.VMEM((N_HEADS, HEAD_DIM, HEAD_DIM), _F32)

    tri3_b, mask_b = _direction_constants(True)
    rev_tile = lambda b, j: (b, nt - 1 - j, 0)
    o_bwd = pl.pallas_call(
        _bwd_kernel,
        grid=(bsz, nt),
        in_specs=[
            pl.BlockSpec((1, tile, D_MODEL), rev_tile),
            _const_spec((1, D_MODEL)),
            _const_spec((D_MODEL, 3 * D_MODEL)),
            _const_spec((2, D_MODEL)),
            _const_spec((CHUNK, 3 * CHUNK)),
            _const_spec((CHUNK, CHUNK)),
        ],
        out_specs=pl.BlockSpec((1, tile, D_MODEL), rev_tile),
        out_shape=jax.ShapeDtypeStruct((bsz, seq, D_MODEL), _F32),
        scratch_shapes=[act_f32, act_f32, act_f32, act_f32, state],
        compiler_params=cparams,
        name="hgrn_bwd_sweep",
    )(x, p["norm_g"], p["w_bwd"], p["lb_bwd"], tri3_b, mask_b)

    tri3_f, mask_f = _direction_constants(False)
    fwd_tile = lambda b, j: (b, j, 0)
    return pl.pallas_call(
        _main_kernel,
        grid=(bsz, nt),
        in_specs=[
            pl.BlockSpec((1, tile, D_MODEL), fwd_tile),
            pl.BlockSpec((1, tile, D_MODEL), fwd_tile),
            _const_spec((1, D_MODEL)),
            _const_spec((D_MODEL, 3 * D_MODEL)),
            _const_spec((D_MODEL, 4 * D_MODEL)),
            _const_spec((1, D_MODEL)),
            _const_spec((1, D_MODEL)),
            _const_spec((N_HEADS, GMLP_CHUNK, GMLP_CHUNK)),
            _const_spec((GMLP_CHUNK, D_MODEL)),
            _const_spec((2, D_MODEL)),
            _const_spec((1, D_MODEL)),
            _const_spec((2 * D_MODEL, D_MODEL)),
            _const_spec((1, D_MODEL)),
            _const_spec((CHUNK, 3 * CHUNK)),
            _const_spec((CHUNK, CHUNK)),
        ],
        out_specs=pl.BlockSpec((1, tile, D_MODEL), fwd_tile),
        out_shape=jax.ShapeDtypeStruct((bsz, seq, D_MODEL), _F32),
        scratch_shapes=[act_f32, act_f32, act_f32, act_f32, act_f32,
                        pltpu.VMEM((tile, 2 * D_MODEL), _BF16), state],
        compiler_params=cparams,
        name="encoder_main_sweep",
    )(x, o_bwd, p["norm_g"], p["w_a"], p["w_b"], p["ln_g"], p["ln_b"], p["w_s"], p["b_s"],
      p["lb_fwd"], p["gn_g"], p["w_out"], p["final_g"], tri3_f, mask_f)


def _prepare(norm_g, w_in, ln_v_g, ln_v_b, w_s, b_s, lb_params, gn_g, w_out, final_g):
    d = D_MODEL
    w = w_in[0].astype(_BF16)
    col = lambda i: w[:, i * d:(i + 1) * d]
    row = lambda a: a.reshape(1, -1).astype(_F32)
    return {
        "norm_g": row(norm_g[0]),
        "w_a": w[:, 0:3 * d],
        "w_b": jnp.concatenate([col(3), col(4), col(6), col(7)], axis=1),
        "w_bwd": jnp.concatenate([col(3), col(5), col(6)], axis=1),
        "ln_g": row(ln_v_g[0]),
        "ln_b": row(ln_v_b[0]),
        "w_s": w_s[0].astype(_BF16),
        "b_s": jnp.repeat(b_s[0].T.astype(_F32), HEAD_DIM, axis=1),
        "lb_fwd": lb_params[0, :, :].astype(_F32),
        "lb_bwd": lb_params[1, :, :].astype(_F32),
        "gn_g": row(gn_g[0]),
        "w_out": w_out[0].astype(_BF16),
        "final_g": row(final_g),
    }


def kernel(x_prompt, x_sample, norm_g, w_in, ln_v_g, ln_v_b, w_s, b_s, lb_params, gn_g, w_out, final_g):
    p = _prepare(norm_g, w_in, ln_v_g, ln_v_b, w_s, b_s, lb_params, gn_g, w_out, final_g)
    return (_trunk(x_prompt, p), _trunk(x_sample, p))
```

```python
import jax
import jax.numpy as jnp
from jax import lax
from jax.experimental import pallas as pl
from jax.experimental.pallas import tpu as pltpu

D_MODEL = 1024
N_HEADS = 8
HEAD_DIM = 128
GMLP_CHUNK = 128
CHUNK = 64
EPS = 1e-6
TILE = 256
VMEM_LIMIT_BYTES = 56 * 1024 * 1024

_BF16 = jnp.bfloat16
_F32 = jnp.float32
_NT = (((1,), (1,)), ((), ()))
_TN = (((0,), (0,)), ((), ()))


def _head(h):
    return slice(h * HEAD_DIM, (h + 1) * HEAD_DIM)


def _sigmoid(x):
    return 1.0 / (1.0 + jnp.exp(-x))


def _rmsnorm_rows(x, gain):
    ms = jnp.mean(x * x, axis=-1, keepdims=True)
    return (x * lax.rsqrt(ms + EPS)) * gain


def _lower_bound(lb_pair):
    m = jnp.max(lb_pair, axis=0, keepdims=True)
    e = jnp.exp(lb_pair - m)
    return e[0:1, :] / jnp.sum(e, axis=0, keepdims=True)


def _cumsum_rows(g, tri3):
    hi = g.astype(_BF16)
    r1 = g - hi.astype(_F32)
    mid = r1.astype(_BF16)
    lo = (r1 - mid.astype(_F32)).astype(_BF16)
    g3 = jnp.concatenate([hi, mid, lo], axis=0)
    return jnp.dot(tri3, g3, preferred_element_type=_F32)


def _prepare_chunk(c, qp, f_logit, v, lb, tri3, end_row, bufs):
    qt_s, kt_s, qi_s, ks_s, v_s, dec_s = bufs
    rows = slice(c * CHUNK, (c + 1) * CHUNK)
    q = qp * _sigmoid(qp)
    fg = lb + (1.0 - lb) * _sigmoid(f_logit)
    g = jnp.log(fg)
    k = 1.0 - fg
    b = _cumsum_rows(g, tri3)
    half = 0.5 * b[end_row:end_row + 1, :]
    d = b - half
    e_half = jnp.exp(half)
    qt = q * jnp.exp(d)
    kt = k * jnp.exp(-d)
    qt_s[rows, :] = qt.astype(_BF16)
    kt_s[rows, :] = kt.astype(_BF16)
    qi_s[rows, :] = (qt * e_half).astype(_BF16)
    ks_s[rows, :] = (kt * e_half).astype(_BF16)
    v_s[rows, :] = v.astype(_BF16)
    dec_s[c:c + 1, :] = e_half * e_half


def _scores_and_states(c, mask, bufs, a_s, sb_s, st_ref):
    qt_s, kt_s, _, ks_s, v_s, dec_s = bufs
    rows = slice(c * CHUNK, (c + 1) * CHUNK)
    for h in range(N_HEADS):
        sl = _head(h)
        scores = lax.dot_general(qt_s[rows, sl], kt_s[rows, sl], _NT, preferred_element_type=_F32)
        a_s[c * N_HEADS + h] = jnp.where(mask, scores, 0.0).astype(_BF16)
        upd = lax.dot_general(v_s[rows, sl], ks_s[rows, sl], _TN, preferred_element_type=_F32)
        st = st_ref[h]
        sb_s[c * N_HEADS + h] = st.astype(_BF16)
        st_ref[h] = st * dec_s[c:c + 1, sl] + upd


def _chunk_output(c, h, bufs, a_s, sb_s):
    _, _, qi_s, _, v_s, _ = bufs
    rows = slice(c * CHUNK, (c + 1) * CHUNK)
    sl = _head(h)
    o = jnp.dot(a_s[c * N_HEADS + h], v_s[rows, sl], preferred_element_type=_F32)
    return o + lax.dot_general(qi_s[rows, sl], sb_s[c * N_HEADS + h], _NT, preferred_element_type=_F32)


def _bwd_kernel(x_ref, ng_ref, w_ref, lb_ref, tri3_ref, mask_ref, o_ref,
                p_s, qt_s, kt_s, qi_s, ks_s, v_s, dec_s, a_s, sb_s, st_ref):
    @pl.when(pl.program_id(1) == 0)
    def _():
        st_ref[...] = jnp.zeros_like(st_ref)

    x = x_ref[0]
    n_chunks = x.shape[0] // CHUNK
    hb = _rmsnorm_rows(x, ng_ref[...]).astype(_BF16)
    p_s[...] = jnp.dot(hb, w_ref[...], preferred_element_type=_F32)
    lb = _lower_bound(lb_ref[...])
    tri3 = tri3_ref[...]
    mask = mask_ref[...] > 0.5
    bufs = (qt_s, kt_s, qi_s, ks_s, v_s, dec_s)
    for c in range(n_chunks):
        rows = slice(c * CHUNK, (c + 1) * CHUNK)
        _prepare_chunk(c, p_s[rows, 0:D_MODEL], p_s[rows, D_MODEL:2 * D_MODEL],
                       p_s[rows, 2 * D_MODEL:3 * D_MODEL], lb, tri3, 0, bufs)
    for c in reversed(range(n_chunks)):
        _scores_and_states(c, mask, bufs, a_s, sb_s, st_ref)
    for c in range(n_chunks):
        rows = slice(c * CHUNK, (c + 1) * CHUNK)
        for h in range(N_HEADS):
            o_ref[0, rows, _head(h)] = _chunk_output(c, h, bufs, a_s, sb_s)


def _main_kernel(x_ref, ob_ref, ng_ref, wa_ref, wb_ref, lng_ref, lnb_ref, ws_ref, bs_ref,
                 lb_ref, gn_ref, wo_ref, fg_ref, tri3_ref, mask_ref, y_ref,
                 p_s, qt_s, kt_s, qi_s, ks_s, v_s, dec_s, a_s, sb_s, z_s, mix_s, st_ref):
    @pl.when(pl.program_id(1) == 0)
    def _():
        st_ref[...] = jnp.zeros_like(st_ref)

    x = x_ref[0]
    tile = x.shape[0]
    n_chunks = tile // CHUNK
    hb = _rmsnorm_rows(x, ng_ref[...]).astype(_BF16)

    pa = jnp.dot(hb, wa_ref[...], preferred_element_type=_F32)
    vv = pa[:, D_MODEL:2 * D_MODEL]
    vc = vv - jnp.mean(vv, axis=-1, keepdims=True)
    vn = vc * lax.rsqrt(jnp.mean(vc * vc, axis=-1, keepdims=True) + EPS)
    vn = (vn * lng_ref[...] + lnb_ref[...]).astype(_BF16)
    za = pa[:, 2 * D_MODEL:3 * D_MODEL]
    gate_a = pa[:, 0:D_MODEL] * (za * _sigmoid(za))
    for n in range(tile // GMLP_CHUNK):
        rs = slice(n * GMLP_CHUNK, (n + 1) * GMLP_CHUNK)
        for h in range(N_HEADS):
            sl = _head(h)
            s = jnp.dot(ws_ref[h], vn[rs, sl], preferred_element_type=_F32) + bs_ref[:, sl]
            mix_s[rs, sl] = (gate_a[rs, sl] * s).astype(_BF16)

    p_s[...] = jnp.dot(hb, wb_ref[...], preferred_element_type=_F32)
    lb = _lower_bound(lb_ref[...])
    tri3 = tri3_ref[...]
    mask = mask_ref[...] > 0.5
    bufs = (qt_s, kt_s, qi_s, ks_s, v_s, dec_s)
    for c in range(n_chunks):
        rows = slice(c * CHUNK, (c + 1) * CHUNK)
        _prepare_chunk(c, p_s[rows, 0:D_MODEL], p_s[rows, D_MODEL:2 * D_MODEL],
                       p_s[rows, 2 * D_MODEL:3 * D_MODEL], lb, tri3, CHUNK - 1, bufs)
        zb = p_s[rows, 3 * D_MODEL:4 * D_MODEL]
        z_s[rows, :] = (zb * _sigmoid(zb)) * gn_ref[...]
    for c in range(n_chunks):
        _scores_and_states(c, mask, bufs, a_s, sb_s, st_ref)
    for c in range(n_chunks):
        rows = slice(c * CHUNK, (c + 1) * CHUNK)
        for h in range(N_HEADS):
            sl = _head(h)
            o = _chunk_output(c, h, bufs, a_s, sb_s) + ob_ref[0, rows, sl]
            o = o * lax.rsqrt(jnp.mean(o * o, axis=-1, keepdims=True) + EPS)
            mix_s[rows, D_MODEL + h * HEAD_DIM:D_MODEL + (h + 1) * HEAD_DIM] = (
                o * z_s[rows, sl]).astype(_BF16)

    out = jnp.dot(mix_s[...], wo_ref[...], preferred_element_type=_F32)
    y_ref[0] = _rmsnorm_rows(x + out, fg_ref[...])


def _const_spec(shape):
    return pl.BlockSpec(shape, lambda b, j: (0,) * len(shape), pipeline_mode=pl.Buffered(1))


def _direction_constants(reverse):
    r = lax.broadcasted_iota(jnp.int32, (CHUNK, CHUNK), 0)
    c = lax.broadcasted_iota(jnp.int32, (CHUNK, CHUNK), 1)
    keep = (c >= r) if reverse else (c <= r)
    mask = keep.astype(_F32)
    tri3 = jnp.concatenate([mask, mask, mask], axis=1).astype(_BF16)
    return tri3, mask


def _recurrence_scratch(tile, proj_cols):
    n_hc = (tile // CHUNK) * N_HEADS
    act_bf16 = pltpu.VMEM((tile, D_MODEL), _BF16)
    return [
        pltpu.VMEM((tile, proj_cols), _F32),
        act_bf16, act_bf16, act_bf16, act_bf16, act_bf16,
        pltpu.VMEM((tile // CHUNK, D_MODEL), _F32),
        pltpu.VMEM((n_hc, CHUNK, CHUNK), _BF16),
        pltpu.VMEM((n_hc, HEAD_DIM, HEAD_DIM), _BF16),
    ]


def _trunk(x, p, tile=TILE):
    bsz, seq, d = x.shape
    assert d == D_MODEL and seq % tile == 0 and tile % GMLP_CHUNK == 0
    nt = seq // tile
    cparams = pltpu.CompilerParams(dimension_semantics=("arbitrary", "arbitrary"),
                                   vmem_limit_bytes=VMEM_LIMIT_BYTES)
    state = pltpu.VMEM((N_HEADS, HEAD_DIM, HEAD_DIM), _F32)

    tri3_b, mask_b = _direction_constants(True)
    rev_tile = lambda b, j: (b, nt - 1 - j, 0)
    o_bwd = pl.pallas_call(
        _bwd_kernel,
        grid=(bsz, nt),
        in_specs=[
            pl.BlockSpec((1, tile, D_MODEL), rev_tile),
            _const_spec((1, D_MODEL)),
            _const_spec((D_MODEL, 3 * D_MODEL)),
            _const_spec((2, D_MODEL)),
            _const_spec((CHUNK, 3 * CHUNK)),
            _const_spec((CHUNK, CHUNK)),
        ],
        out_specs=pl.BlockSpec((1, tile, D_MODEL), rev_tile),
        out_shape=jax.ShapeDtypeStruct((bsz, seq, D_MODEL), _F32),
        scratch_shapes=_recurrence_scratch(tile, 3 * D_MODEL) + [state],
        compiler_params=cparams,
        name="hgrn_bwd_sweep",
    )(x, p["norm_g"], p["w_bwd"], p["lb_bwd"], tri3_b, mask_b)

    tri3_f, mask_f = _direction_constants(False)
    fwd_tile = lambda b, j: (b, j, 0)
    return pl.pallas_call(
        _main_kernel,
        grid=(bsz, nt),
        in_specs=[
            pl.BlockSpec((1, tile, D_MODEL), fwd_tile),
            pl.BlockSpec((1, tile, D_MODEL), fwd_tile),
            _const_spec((1, D_MODEL)),
            _const_spec((D_MODEL, 3 * D_MODEL)),
            _const_spec((D_MODEL, 4 * D_MODEL)),
            _const_spec((1, D_MODEL)),
            _const_spec((1, D_MODEL)),
            _const_spec((N_HEADS, GMLP_CHUNK, GMLP_CHUNK)),
            _const_spec((GMLP_CHUNK, D_MODEL)),
            _const_spec((2, D_MODEL)),
            _const_spec((1, D_MODEL)),
            _const_spec((2 * D_MODEL, D_MODEL)),
            _const_spec((1, D_MODEL)),
            _const_spec((CHUNK, 3 * CHUNK)),
            _const_spec((CHUNK, CHUNK)),
        ],
        out_specs=pl.BlockSpec((1, tile, D_MODEL), fwd_tile),
        out_shape=jax.ShapeDtypeStruct((bsz, seq, D_MODEL), _F32),
        scratch_shapes=_recurrence_scratch(tile, 4 * D_MODEL) + [
            pltpu.VMEM((tile, D_MODEL), _F32),
            pltpu.VMEM((tile, 2 * D_MODEL), _BF16),
            state],
        compiler_params=cparams,
        name="encoder_main_sweep",
    )(x, o_bwd, p["norm_g"], p["w_a"], p["w_b"], p["ln_g"], p["ln_b"], p["w_s"], p["b_s"],
      p["lb_fwd"], p["gn_g"], p["w_out"], p["final_g"], tri3_f, mask_f)


def _prepare(norm_g, w_in, ln_v_g, ln_v_b, w_s, b_s, lb_params, gn_g, w_out, final_g):
    d = D_MODEL
    w = w_in[0].astype(_BF16)
    col = lambda i: w[:, i * d:(i + 1) * d]
    row = lambda a: a.reshape(1, -1).astype(_F32)
    return {
        "norm_g": row(norm_g[0]),
        "w_a": w[:, 0:3 * d],
        "w_b": jnp.concatenate([col(3), col(4), col(6), col(7)], axis=1),
        "w_bwd": jnp.concatenate([col(3), col(5), col(6)], axis=1),
        "ln_g": row(ln_v_g[0]),
        "ln_b": row(ln_v_b[0]),
        "w_s": w_s[0].astype(_BF16),
        "b_s": jnp.repeat(b_s[0].T.astype(_F32), HEAD_DIM, axis=1),
        "lb_fwd": lb_params[0, :, :].astype(_F32),
        "lb_bwd": lb_params[1, :, :].astype(_F32),
        "gn_g": row(gn_g[0]),
        "w_out": w_out[0].astype(_BF16),
        "final_g": row(final_g),
    }


def kernel(x_prompt, x_sample, norm_g, w_in, ln_v_g, ln_v_b, w_s, b_s, lb_params, gn_g, w_out, final_g):
    p = _prepare(norm_g, w_in, ln_v_g, ln_v_b, w_s, b_s, lb_params, gn_g, w_out, final_g)
    return (_trunk(x_prompt, p), _trunk(x_sample, p))
```

```python
import jax
import jax.numpy as jnp
from jax import lax
from jax.experimental import pallas as pl
from jax.experimental.pallas import tpu as pltpu

D_MODEL = 1024
N_HEADS = 8
HEAD_DIM = 128
GMLP_CHUNK = 128
CHUNK = 64
EPS = 1e-6
TILE = 256
VMEM_LIMIT_BYTES = 56 * 1024 * 1024

_BF16 = jnp.bfloat16
_F32 = jnp.float32
_NT = (((1,), (1,)), ((), ()))
_TN = (((0,), (0,)), ((), ()))


def _head(h):
    return slice(h * HEAD_DIM, (h + 1) * HEAD_DIM)


def _sigmoid(x):
    return 1.0 / (1.0 + jnp.exp(-x))


def _rmsnorm_rows(x, gain):
    ms = jnp.mean(x * x, axis=-1, keepdims=True)
    return (x * lax.rsqrt(ms + EPS)) * gain


def _lower_bound(lb_pair):
    m = jnp.max(lb_pair, axis=0, keepdims=True)
    e = jnp.exp(lb_pair - m)
    return e[0:1, :] / jnp.sum(e, axis=0, keepdims=True)


def _cumsum_rows(g, tri3):
    hi = g.astype(_BF16)
    r1 = g - hi.astype(_F32)
    mid = r1.astype(_BF16)
    lo = (r1 - mid.astype(_F32)).astype(_BF16)
    g3 = jnp.concatenate([hi, mid, lo], axis=0)
    return jnp.dot(tri3, g3, preferred_element_type=_F32)


def _prepare_chunk(c, q, f_logit, lb, tri3, end_row, bufs):
    qt_s, kt_s, qi_s, ks_s, dec_s = bufs
    rows = slice(c * CHUNK, (c + 1) * CHUNK)
    fg = lb + (1.0 - lb) * _sigmoid(f_logit)
    g = jnp.log(fg)
    k = 1.0 - fg
    b = _cumsum_rows(g, tri3)
    half = 0.5 * b[end_row:end_row + 1, :]
    d = b - half
    e_half = jnp.exp(half)
    qt = q * jnp.exp(d)
    kt = k * jnp.exp(-d)
    qt_s[rows, :] = qt.astype(_BF16)
    kt_s[rows, :] = kt.astype(_BF16)
    qi_s[rows, :] = (qt * e_half).astype(_BF16)
    ks_s[rows, :] = (kt * e_half).astype(_BF16)
    dec_s[c:c + 1, :] = e_half * e_half


def _scores_and_states(c, mask, bufs, v_ref, a_s, sb_s, st_ref):
    qt_s, kt_s, _, ks_s, dec_s = bufs
    rows = slice(c * CHUNK, (c + 1) * CHUNK)
    for h in range(N_HEADS):
        sl = _head(h)
        scores = lax.dot_general(qt_s[rows, sl], kt_s[rows, sl], _NT, preferred_element_type=_F32)
        a_s[c * N_HEADS + h] = jnp.where(mask, scores, 0.0).astype(_BF16)
        upd = lax.dot_general(v_ref[0, rows, sl], ks_s[rows, sl], _TN, preferred_element_type=_F32)
        st = st_ref[h]
        sb_s[c * N_HEADS + h] = st.astype(_BF16)
        st_ref[h] = st * dec_s[c:c + 1, sl] + upd


def _chunk_output(c, h, bufs, v_ref, a_s, sb_s):
    _, _, qi_s, _, _ = bufs
    rows = slice(c * CHUNK, (c + 1) * CHUNK)
    sl = _head(h)
    o = jnp.dot(a_s[c * N_HEADS + h], v_ref[0, rows, sl], preferred_element_type=_F32)
    return o + lax.dot_general(qi_s[rows, sl], sb_s[c * N_HEADS + h], _NT, preferred_element_type=_F32)


def _bwd_kernel(x_ref, ng_ref, w_ref, lb_ref, tri3_ref, mask_ref, o_ref, q_ref, v_ref,
                p_s, qt_s, kt_s, qi_s, ks_s, dec_s, a_s, sb_s, st_ref):
    @pl.when(pl.program_id(1) == 0)
    def _():
        st_ref[...] = jnp.zeros_like(st_ref)

    x = x_ref[0]
    n_chunks = x.shape[0] // CHUNK
    hb = _rmsnorm_rows(x, ng_ref[...]).astype(_BF16)
    p_s[...] = jnp.dot(hb, w_ref[...], preferred_element_type=_F32)
    lb = _lower_bound(lb_ref[...])
    tri3 = tri3_ref[...]
    mask = mask_ref[...] > 0.5
    bufs = (qt_s, kt_s, qi_s, ks_s, dec_s)
    for c in range(n_chunks):
        rows = slice(c * CHUNK, (c + 1) * CHUNK)
        qp = p_s[rows, 0:D_MODEL]
        q = qp * _sigmoid(qp)
        q_ref[0, rows, :] = q
        v_ref[0, rows, :] = p_s[rows, 2 * D_MODEL:3 * D_MODEL].astype(_BF16)
        _prepare_chunk(c, q, p_s[rows, D_MODEL:2 * D_MODEL], lb, tri3, 0, bufs)
    for c in reversed(range(n_chunks)):
        _scores_and_states(c, mask, bufs, v_ref, a_s, sb_s, st_ref)
    for c in range(n_chunks):
        rows = slice(c * CHUNK, (c + 1) * CHUNK)
        for h in range(N_HEADS):
            o_ref[0, rows, _head(h)] = _chunk_output(c, h, bufs, v_ref, a_s, sb_s)


def _main_kernel(x_ref, ob_ref, q_ref, v_ref, ng_ref, wa_ref, wb_ref, lng_ref, lnb_ref, ws_ref,
                 bs_ref, lb_ref, gn_ref, wo_ref, fg_ref, tri3_ref, mask_ref, y_ref,
                 p_s, qt_s, kt_s, qi_s, ks_s, dec_s, a_s, sb_s, z_s, mix_s, st_ref):
    @pl.when(pl.program_id(1) == 0)
    def _():
        st_ref[...] = jnp.zeros_like(st_ref)

    x = x_ref[0]
    tile = x.shape[0]
    n_chunks = tile // CHUNK
    hb = _rmsnorm_rows(x, ng_ref[...]).astype(_BF16)

    pa = jnp.dot(hb, wa_ref[...], preferred_element_type=_F32)
    vv = pa[:, D_MODEL:2 * D_MODEL]
    vc = vv - jnp.mean(vv, axis=-1, keepdims=True)
    vn = vc * lax.rsqrt(jnp.mean(vc * vc, axis=-1, keepdims=True) + EPS)
    vn = (vn * lng_ref[...] + lnb_ref[...]).astype(_BF16)
    za = pa[:, 2 * D_MODEL:3 * D_MODEL]
    gate_a = pa[:, 0:D_MODEL] * (za * _sigmoid(za))
    for n in range(tile // GMLP_CHUNK):
        rs = slice(n * GMLP_CHUNK, (n + 1) * GMLP_CHUNK)
        for h in range(N_HEADS):
            sl = _head(h)
            s = jnp.dot(ws_ref[h], vn[rs, sl], preferred_element_type=_F32) + bs_ref[:, sl]
            mix_s[rs, sl] = (gate_a[rs, sl] * s).astype(_BF16)

    p_s[...] = jnp.dot(hb, wb_ref[...], preferred_element_type=_F32)
    lb = _lower_bound(lb_ref[...])
    tri3 = tri3_ref[...]
    mask = mask_ref[...] > 0.5
    bufs = (qt_s, kt_s, qi_s, ks_s, dec_s)
    for c in range(n_chunks):
        rows = slice(c * CHUNK, (c + 1) * CHUNK)
        _prepare_chunk(c, q_ref[0, rows, :], p_s[rows, 0:D_MODEL], lb, tri3, CHUNK - 1, bufs)
        zb = p_s[rows, D_MODEL:2 * D_MODEL]
        z_s[rows, :] = (zb * _sigmoid(zb)) * gn_ref[...]
    for c in range(n_chunks):
        _scores_and_states(c, mask, bufs, v_ref, a_s, sb_s, st_ref)
    for c in range(n_chunks):
        rows = slice(c * CHUNK, (c + 1) * CHUNK)
        for h in range(N_HEADS):
            sl = _head(h)
            o = _chunk_output(c, h, bufs, v_ref, a_s, sb_s) + ob_ref[0, rows, sl]
            o = o * lax.rsqrt(jnp.mean(o * o, axis=-1, keepdims=True) + EPS)
            mix_s[rows, D_MODEL + h * HEAD_DIM:D_MODEL + (h + 1) * HEAD_DIM] = (
                o * z_s[rows, sl]).astype(_BF16)

    out = jnp.dot(mix_s[...], wo_ref[...], preferred_element_type=_F32)
    y_ref[0] = _rmsnorm_rows(x + out, fg_ref[...])


def _const_spec(shape):
    return pl.BlockSpec(shape, lambda b, j: (0,) * len(shape), pipeline_mode=pl.Buffered(1))


def _direction_constants(reverse):
    r = lax.broadcasted_iota(jnp.int32, (CHUNK, CHUNK), 0)
    c = lax.broadcasted_iota(jnp.int32, (CHUNK, CHUNK), 1)
    keep = (c >= r) if reverse else (c <= r)
    mask = keep.astype(_F32)
    tri3 = jnp.concatenate([mask, mask, mask], axis=1).astype(_BF16)
    return tri3, mask


def _recurrence_scratch(tile, proj_cols):
    n_hc = (tile // CHUNK) * N_HEADS
    act_bf16 = pltpu.VMEM((tile, D_MODEL), _BF16)
    return [
        pltpu.VMEM((tile, proj_cols), _F32),
        act_bf16, act_bf16, act_bf16, act_bf16,
        pltpu.VMEM((tile // CHUNK, D_MODEL), _F32),
        pltpu.VMEM((n_hc, CHUNK, CHUNK), _BF16),
        pltpu.VMEM((n_hc, HEAD_DIM, HEAD_DIM), _BF16),
    ]


def _trunk(x, p, tile=TILE):
    bsz, seq, d = x.shape
    assert d == D_MODEL and seq % tile == 0 and tile % GMLP_CHUNK == 0
    nt = seq // tile
    cparams = pltpu.CompilerParams(dimension_semantics=("arbitrary", "arbitrary"),
                                   vmem_limit_bytes=VMEM_LIMIT_BYTES)
    state = pltpu.VMEM((N_HEADS, HEAD_DIM, HEAD_DIM), _F32)

    tri3_b, mask_b = _direction_constants(True)
    rev_tile = lambda b, j: (b, nt - 1 - j, 0)
    act = lambda dtype: jax.ShapeDtypeStruct((bsz, seq, D_MODEL), dtype)
    o_bwd, q_act, v_act = pl.pallas_call(
        _bwd_kernel,
        grid=(bsz, nt),
        in_specs=[
            pl.BlockSpec((1, tile, D_MODEL), rev_tile),
            _const_spec((1, D_MODEL)),
            _const_spec((D_MODEL, 3 * D_MODEL)),
            _const_spec((2, D_MODEL)),
            _const_spec((CHUNK, 3 * CHUNK)),
            _const_spec((CHUNK, CHUNK)),
        ],
        out_specs=[pl.BlockSpec((1, tile, D_MODEL), rev_tile)] * 3,
        out_shape=[act(_F32), act(_F32), act(_BF16)],
        scratch_shapes=_recurrence_scratch(tile, 3 * D_MODEL) + [state],
        compiler_params=cparams,
        name="hgrn_bwd_sweep",
    )(x, p["norm_g"], p["w_bwd"], p["lb_bwd"], tri3_b, mask_b)

    tri3_f, mask_f = _direction_constants(False)
    fwd_tile = lambda b, j: (b, j, 0)
    return pl.pallas_call(
        _main_kernel,
        grid=(bsz, nt),
        in_specs=[
            pl.BlockSpec((1, tile, D_MODEL), fwd_tile),
            pl.BlockSpec((1, tile, D_MODEL), fwd_tile),
            pl.BlockSpec((1, tile, D_MODEL), fwd_tile),
            pl.BlockSpec((1, tile, D_MODEL), fwd_tile),
            _const_spec((1, D_MODEL)),
            _const_spec((D_MODEL, 3 * D_MODEL)),
            _const_spec((D_MODEL, 2 * D_MODEL)),
            _const_spec((1, D_MODEL)),
            _const_spec((1, D_MODEL)),
            _const_spec((N_HEADS, GMLP_CHUNK, GMLP_CHUNK)),
            _const_spec((GMLP_CHUNK, D_MODEL)),
            _const_spec((2, D_MODEL)),
            _const_spec((1, D_MODEL)),
            _const_spec((2 * D_MODEL, D_MODEL)),
            _const_spec((1, D_MODEL)),
            _const_spec((CHUNK, 3 * CHUNK)),
            _const_spec((CHUNK, CHUNK)),
        ],
        out_specs=pl.BlockSpec((1, tile, D_MODEL), fwd_tile),
        out_shape=jax.ShapeDtypeStruct((bsz, seq, D_MODEL), _F32),
        scratch_shapes=_recurrence_scratch(tile, 2 * D_MODEL) + [
            pltpu.VMEM((tile, D_MODEL), _F32),
            pltpu.VMEM((tile, 2 * D_MODEL), _BF16),
            state],
        compiler_params=cparams,
        name="encoder_main_sweep",
    )(x, o_bwd, q_act, v_act, p["norm_g"], p["w_a"], p["w_b"], p["ln_g"], p["ln_b"], p["w_s"], p["b_s"],
      p["lb_fwd"], p["gn_g"], p["w_out"], p["final_g"], tri3_f, mask_f)


def _prepare(norm_g, w_in, ln_v_g, ln_v_b, w_s, b_s, lb_params, gn_g, w_out, final_g):
    d = D_MODEL
    w = w_in[0].astype(_BF16)
    col = lambda i: w[:, i * d:(i + 1) * d]
    row = lambda a: a.reshape(1, -1).astype(_F32)
    return {
        "norm_g": row(norm_g[0]),
        "w_a": w[:, 0:3 * d],
        "w_b": jnp.concatenate([col(4), col(7)], axis=1),
        "w_bwd": jnp.concatenate([col(3), col(5), col(6)], axis=1),
        "ln_g": row(ln_v_g[0]),
        "ln_b": row(ln_v_b[0]),
        "w_s": w_s[0].astype(_BF16),
        "b_s": jnp.repeat(b_s[0].T.astype(_F32), HEAD_DIM, axis=1),
        "lb_fwd": lb_params[0, :, :].astype(_F32),
        "lb_bwd": lb_params[1, :, :].astype(_F32),
        "gn_g": row(gn_g[0]),
        "w_out": w_out[0].astype(_BF16),
        "final_g": row(final_g),
    }


def kernel(x_prompt, x_sample, norm_g, w_in, ln_v_g, ln_v_b, w_s, b_s, lb_params, gn_g, w_out, final_g):
    p = _prepare(norm_g, w_in, ln_v_g, ln_v_b, w_s, b_s, lb_params, gn_g, w_out, final_g)
    return (_trunk(x_prompt, p), _trunk(x_sample, p))
```

```python
import jax
import jax.numpy as jnp
from jax import lax
from jax.experimental import pallas as pl
from jax.experimental.pallas import tpu as pltpu

D_MODEL = 1024
N_HEADS = 8
HEAD_DIM = 128
GMLP_CHUNK = 128
CHUNK = 64
EPS = 1e-6
TILE = 256
VMEM_LIMIT_BYTES = 56 * 1024 * 1024

_BF16 = jnp.bfloat16
_F32 = jnp.float32
_NT = (((1,), (1,)), ((), ()))
_TN = (((0,), (0,)), ((), ()))


def _head(h):
    return slice(h * HEAD_DIM, (h + 1) * HEAD_DIM)


def _sigmoid(x):
    return 1.0 / (1.0 + jnp.exp(-x))


def _rmsnorm_rows(x, gain):
    ms = jnp.mean(x * x, axis=-1, keepdims=True)
    return (x * lax.rsqrt(ms + EPS)) * gain


def _lower_bound(lb_pair):
    m = jnp.max(lb_pair, axis=0, keepdims=True)
    e = jnp.exp(lb_pair - m)
    return e[0:1, :] / jnp.sum(e, axis=0, keepdims=True)


def _cumsum_rows(g, tri3):
    hi = g.astype(_BF16)
    r1 = g - hi.astype(_F32)
    mid = r1.astype(_BF16)
    lo = (r1 - mid.astype(_F32)).astype(_BF16)
    g3 = jnp.concatenate([hi, mid, lo], axis=0)
    return jnp.dot(tri3, g3, preferred_element_type=_F32)


def _prepare_keys(c, f_logit, lb, tri3, end_row, bufs):
    _, kt_s, ks_s, ed_s, eh_s, _, _ = bufs
    rows = slice(c * CHUNK, (c + 1) * CHUNK)
    fg = lb + (1.0 - lb) * _sigmoid(f_logit)
    g = jnp.log(fg)
    k = 1.0 - fg
    b = _cumsum_rows(g, tri3)
    half = 0.5 * b[end_row:end_row + 1, :]
    d = b - half
    e_half = jnp.exp(half)
    kt = k * jnp.exp(-d)
    kt_s[rows, :] = kt.astype(_BF16)
    ks_s[rows, :] = (kt * e_half).astype(_BF16)
    ed_s[rows, :] = jnp.exp(d)
    eh_s[c:c + 1, :] = e_half


def _prepare_queries(c, q, bufs):
    qt_s, _, _, ed_s, eh_s, qa_s, _ = bufs
    rows = slice(c * CHUNK, (c + 1) * CHUNK)
    qt = q * ed_s[rows, :]
    qt_s[rows, :] = qt.astype(_BF16)
    qi = (qt * eh_s[c:c + 1, :]).astype(_BF16)
    for h in range(N_HEADS):
        qa_s[c * N_HEADS + h, :, 0:HEAD_DIM] = qi[:, _head(h)]


def _scores_and_states(c, mask, bufs, v_ref, st_ref):
    qt_s, kt_s, ks_s, _, eh_s, qa_s, sv_s = bufs
    rows = slice(c * CHUNK, (c + 1) * CHUNK)
    e_half = eh_s[c:c + 1, :]
    decay = e_half * e_half
    for h in range(N_HEADS):
        sl = _head(h)
        i = c * N_HEADS + h
        scores = lax.dot_general(qt_s[rows, sl], kt_s[rows, sl], _NT, preferred_element_type=_F32)
        qa_s[i, :, HEAD_DIM:HEAD_DIM + CHUNK] = jnp.where(mask, scores, 0.0).astype(_BF16)
        v = v_ref[0, rows, sl]
        upd = lax.dot_general(v, ks_s[rows, sl], _TN, preferred_element_type=_F32)
        st = st_ref[h]
        sv_s[i, 0:HEAD_DIM, :] = st.T.astype(_BF16)
        sv_s[i, HEAD_DIM:HEAD_DIM + CHUNK, :] = v
        st_ref[h] = st * decay[:, sl] + upd


def _chunk_output(c, h, bufs):
    qa_s, sv_s = bufs[-2:]
    i = c * N_HEADS + h
    return jnp.dot(qa_s[i], sv_s[i], preferred_element_type=_F32)


def _bwd_kernel(x_ref, ng_ref, w_ref, lb_ref, tri3_ref, mask_ref, o_ref, q_ref, v_ref,
                p_s, qt_s, kt_s, ks_s, ed_s, eh_s, qa_s, sv_s, st_ref):
    @pl.when(pl.program_id(1) == 0)
    def _():
        st_ref[...] = jnp.zeros_like(st_ref)

    x = x_ref[0]
    n_chunks = x.shape[0] // CHUNK
    hb = _rmsnorm_rows(x, ng_ref[...]).astype(_BF16)
    lb = _lower_bound(lb_ref[...])
    tri3 = tri3_ref[...]
    mask = mask_ref[...] > 0.5
    bufs = (qt_s, kt_s, ks_s, ed_s, eh_s, qa_s, sv_s)
    p_s[...] = jnp.dot(hb, w_ref[:, D_MODEL:2 * D_MODEL], preferred_element_type=_F32)
    qp = jnp.dot(hb, w_ref[:, 0:D_MODEL], preferred_element_type=_F32)
    q_ref[0] = qp * _sigmoid(qp)
    v_ref[0] = jnp.dot(hb, w_ref[:, 2 * D_MODEL:3 * D_MODEL], preferred_element_type=_F32).astype(_BF16)
    for c in range(n_chunks):
        _prepare_keys(c, p_s[c * CHUNK:(c + 1) * CHUNK, :], lb, tri3, 0, bufs)
    for c in range(n_chunks):
        _prepare_queries(c, q_ref[0, c * CHUNK:(c + 1) * CHUNK, :], bufs)
    for c in reversed(range(n_chunks)):
        _scores_and_states(c, mask, bufs, v_ref, st_ref)
    for c in range(n_chunks):
        rows = slice(c * CHUNK, (c + 1) * CHUNK)
        for h in range(N_HEADS):
            o_ref[0, rows, _head(h)] = _chunk_output(c, h, bufs)


def _main_kernel(x_ref, ob_ref, q_ref, v_ref, ng_ref, wa_ref, wb_ref, lng_ref, lnb_ref, ws_ref,
                 bs_ref, lb_ref, gn_ref, wo_ref, fg_ref, tri3_ref, mask_ref, y_ref,
                 p_s, qt_s, kt_s, ks_s, ed_s, eh_s, qa_s, sv_s, z_s, mix_s, st_ref):
    @pl.when(pl.program_id(1) == 0)
    def _():
        st_ref[...] = jnp.zeros_like(st_ref)

    x = x_ref[0]
    tile = x.shape[0]
    n_chunks = tile // CHUNK
    hb = _rmsnorm_rows(x, ng_ref[...]).astype(_BF16)
    lb = _lower_bound(lb_ref[...])
    tri3 = tri3_ref[...]
    mask = mask_ref[...] > 0.5
    bufs = (qt_s, kt_s, ks_s, ed_s, eh_s, qa_s, sv_s)

    p_s[...] = jnp.dot(hb, wb_ref[:, 0:D_MODEL], preferred_element_type=_F32)

    pa = jnp.dot(hb, wa_ref[...], preferred_element_type=_F32)
    for c in range(n_chunks):
        _prepare_keys(c, p_s[c * CHUNK:(c + 1) * CHUNK, :], lb, tri3, CHUNK - 1, bufs)
    vv = pa[:, D_MODEL:2 * D_MODEL]
    vc = vv - jnp.mean(vv, axis=-1, keepdims=True)
    vn = vc * lax.rsqrt(jnp.mean(vc * vc, axis=-1, keepdims=True) + EPS)
    vn = (vn * lng_ref[...] + lnb_ref[...]).astype(_BF16)
    za = pa[:, 2 * D_MODEL:3 * D_MODEL]
    gate_a = pa[:, 0:D_MODEL] * (za * _sigmoid(za))
    for n in range(tile // GMLP_CHUNK):
        rs = slice(n * GMLP_CHUNK, (n + 1) * GMLP_CHUNK)
        for h in range(N_HEADS):
            sl = _head(h)
            s = jnp.dot(ws_ref[h], vn[rs, sl], preferred_element_type=_F32) + bs_ref[:, sl]
            mix_s[rs, sl] = (gate_a[rs, sl] * s).astype(_BF16)

    zb = jnp.dot(hb, wb_ref[:, D_MODEL:2 * D_MODEL], preferred_element_type=_F32)
    z_s[...] = (zb * _sigmoid(zb)) * gn_ref[...]
    for c in range(n_chunks):
        _prepare_queries(c, q_ref[0, c * CHUNK:(c + 1) * CHUNK, :], bufs)
    for c in range(n_chunks):
        _scores_and_states(c, mask, bufs, v_ref, st_ref)
    for c in range(n_chunks):
        rows = slice(c * CHUNK, (c + 1) * CHUNK)
        for h in range(N_HEADS):
            sl = _head(h)
            o = _chunk_output(c, h, bufs) + ob_ref[0, rows, sl]
            o = o * lax.rsqrt(jnp.mean(o * o, axis=-1, keepdims=True) + EPS)
            mix_s[rows, D_MODEL + h * HEAD_DIM:D_MODEL + (h + 1) * HEAD_DIM] = (
                o * z_s[rows, sl]).astype(_BF16)

    out = jnp.dot(mix_s[...], wo_ref[...], preferred_element_type=_F32)
    y_ref[0] = _rmsnorm_rows(x + out, fg_ref[...])


def _const_spec(shape):
    return pl.BlockSpec(shape, lambda b, j: (0,) * len(shape), pipeline_mode=pl.Buffered(1))


def _direction_constants(reverse):
    r = lax.broadcasted_iota(jnp.int32, (CHUNK, CHUNK), 0)
    c = lax.broadcasted_iota(jnp.int32, (CHUNK, CHUNK), 1)
    keep = (c >= r) if reverse else (c <= r)
    mask = keep.astype(_F32)
    tri3 = jnp.concatenate([mask, mask, mask], axis=1).astype(_BF16)
    return tri3, mask


def _recurrence_scratch(tile):
    n_hc = (tile // CHUNK) * N_HEADS
    act_bf16 = pltpu.VMEM((tile, D_MODEL), _BF16)
    return [
        pltpu.VMEM((tile, D_MODEL), _F32),
        act_bf16, act_bf16, act_bf16,
        pltpu.VMEM((tile, D_MODEL), _F32),
        pltpu.VMEM((tile // CHUNK, D_MODEL), _F32),
        pltpu.VMEM((n_hc, CHUNK, HEAD_DIM + CHUNK), _BF16),
        pltpu.VMEM((n_hc, HEAD_DIM + CHUNK, HEAD_DIM), _BF16),
    ]


def _trunk(x, p, tile=TILE):
    bsz, seq, d = x.shape
    assert d == D_MODEL and seq % tile == 0 and tile % GMLP_CHUNK == 0
    nt = seq // tile
    cparams = pltpu.CompilerParams(dimension_semantics=("arbitrary", "arbitrary"),
                                   vmem_limit_bytes=VMEM_LIMIT_BYTES)
    state = pltpu.VMEM((N_HEADS, HEAD_DIM, HEAD_DIM), _F32)

    tri3_b, mask_b = _direction_constants(True)
    rev_tile = lambda b, j: (b, nt - 1 - j, 0)
    act = lambda dtype: jax.ShapeDtypeStruct((bsz, seq, D_MODEL), dtype)
    o_bwd, q_act, v_act = pl.pallas_call(
        _bwd_kernel,
        grid=(bsz, nt),
        in_specs=[
            pl.BlockSpec((1, tile, D_MODEL), rev_tile),
            _const_spec((1, D_MODEL)),
            _const_spec((D_MODEL, 3 * D_MODEL)),
            _const_spec((2, D_MODEL)),
            _const_spec((CHUNK, 3 * CHUNK)),
            _const_spec((CHUNK, CHUNK)),
        ],
        out_specs=[pl.BlockSpec((1, tile, D_MODEL), rev_tile)] * 3,
        out_shape=[act(_F32), act(_F32), act(_BF16)],
        scratch_shapes=_recurrence_scratch(tile) + [state],
        compiler_params=cparams,
        name="hgrn_bwd_sweep",
    )(x, p["norm_g"], p["w_bwd"], p["lb_bwd"], tri3_b, mask_b)

    tri3_f, mask_f = _direction_constants(False)
    fwd_tile = lambda b, j: (b, j, 0)
    return pl.pallas_call(
        _main_kernel,
        grid=(bsz, nt),
        in_specs=[
            pl.BlockSpec((1, tile, D_MODEL), fwd_tile),
            pl.BlockSpec((1, tile, D_MODEL), fwd_tile),
            pl.BlockSpec((1, tile, D_MODEL), fwd_tile),
            pl.BlockSpec((1, tile, D_MODEL), fwd_tile),
            _const_spec((1, D_MODEL)),
            _const_spec((D_MODEL, 3 * D_MODEL)),
            _const_spec((D_MODEL, 2 * D_MODEL)),
            _const_spec((1, D_MODEL)),
            _const_spec((1, D_MODEL)),
            _const_spec((N_HEADS, GMLP_CHUNK, GMLP_CHUNK)),
            _const_spec((GMLP_CHUNK, D_MODEL)),
            _const_spec((2, D_MODEL)),
            _const_spec((1, D_MODEL)),
            _const_spec((2 * D_MODEL, D_MODEL)),
            _const_spec((1, D_MODEL)),
            _const_spec((CHUNK, 3 * CHUNK)),
            _const_spec((CHUNK, CHUNK)),
        ],
        out_specs=pl.BlockSpec((1, tile, D_MODEL), fwd_tile),
        out_shape=jax.ShapeDtypeStruct((bsz, seq, D_MODEL), _F32),
        scratch_shapes=_recurrence_scratch(tile) + [
            pltpu.VMEM((tile, D_MODEL), _F32),
            pltpu.VMEM((tile, 2 * D_MODEL), _BF16),
            state],
        compiler_params=cparams,
        name="encoder_main_sweep",
    )(x, o_bwd, q_act, v_act, p["norm_g"], p["w_a"], p["w_b"], p["ln_g"], p["ln_b"], p["w_s"], p["b_s"],
      p["lb_fwd"], p["gn_g"], p["w_out"], p["final_g"], tri3_f, mask_f)


def _prepare(norm_g, w_in, ln_v_g, ln_v_b, w_s, b_s, lb_params, gn_g, w_out, final_g):
    d = D_MODEL
    w = w_in[0].astype(_BF16)
    col = lambda i: w[:, i * d:(i + 1) * d]
    row = lambda a: a.reshape(1, -1).astype(_F32)
    return {
        "norm_g": row(norm_g[0]),
        "w_a": w[:, 0:3 * d],
        "w_b": jnp.concatenate([col(4), col(7)], axis=1),
        "w_bwd": jnp.concatenate([col(3), col(5), col(6)], axis=1),
        "ln_g": row(ln_v_g[0]),
        "ln_b": row(ln_v_b[0]),
        "w_s": w_s[0].astype(_BF16),
        "b_s": jnp.repeat(b_s[0].T.astype(_F32), HEAD_DIM, axis=1),
        "lb_fwd": lb_params[0, :, :].astype(_F32),
        "lb_bwd": lb_params[1, :, :].astype(_F32),
        "gn_g": row(gn_g[0]),
        "w_out": w_out[0].astype(_BF16),
        "final_g": row(final_g),
    }


def kernel(x_prompt, x_sample, norm_g, w_in, ln_v_g, ln_v_b, w_s, b_s, lb_params, gn_g, w_out, final_g):
    p = _prepare(norm_g, w_in, ln_v_g, ln_v_b, w_s, b_s, lb_params, gn_g, w_out, final_g)
    return (_trunk(x_prompt, p), _trunk(x_sample, p))
```

```python
import jax
import jax.numpy as jnp
from jax import lax
from jax.experimental import pallas as pl
from jax.experimental.pallas import tpu as pltpu

D_MODEL = 1024
N_HEADS = 8
HEAD_DIM = 128
GMLP_CHUNK = 128
CHUNK = 64
EPS = 1e-6
TILE = 256
TILE_BWD = 256
VMEM_LIMIT_BYTES = 56 * 1024 * 1024

_BF16 = jnp.bfloat16
_F32 = jnp.float32
_LOG2E = 1.4426950408889634
_NT = (((1,), (1,)), ((), ()))
_TN = (((0,), (0,)), ((), ()))


def _head(h):
    return slice(h * HEAD_DIM, (h + 1) * HEAD_DIM)


def _exp(x, sign=1.0):
    return jnp.exp2(x * (sign * _LOG2E))


def _sigmoid(x):
    return 1.0 / (1.0 + _exp(x, -1.0))


def _rmsnorm_rows(x, gain):
    ms = jnp.mean(x * x, axis=-1, keepdims=True)
    return (x * lax.rsqrt(ms + EPS)) * gain


def _lower_bound(lb_pair):
    m = jnp.max(lb_pair, axis=0, keepdims=True)
    e = jnp.exp(lb_pair - m)
    return e[0:1, :] / jnp.sum(e, axis=0, keepdims=True)


def _cumsum_rows(g, tri3):
    hi = g.astype(_BF16)
    r1 = g - hi.astype(_F32)
    mid = r1.astype(_BF16)
    lo = (r1 - mid.astype(_F32)).astype(_BF16)
    g3 = jnp.concatenate([hi, mid, lo], axis=0)
    return jnp.dot(tri3, g3, preferred_element_type=_F32)


def _prepare_keys(c, f_logit, lb, tri3, end_row, bufs):
    _, kt_s, ks_s, ed_s, eh_s, _, _ = bufs
    rows = slice(c * CHUNK, (c + 1) * CHUNK)
    fg = lb + (1.0 - lb) * _sigmoid(f_logit)
    g = jnp.log(fg)
    k = 1.0 - fg
    b = _cumsum_rows(g, tri3)
    half = 0.5 * b[end_row:end_row + 1, :]
    d = b - half
    e_half = _exp(half)
    kt = k * _exp(d, -1.0)
    kt_s[rows, :] = kt.astype(_BF16)
    ks_s[rows, :] = (kt * e_half).astype(_BF16)
    ed_s[rows, :] = _exp(d)
    eh_s[c:c + 1, :] = e_half


def _prepare_queries(c, q, bufs):
    qt_s, _, _, ed_s, eh_s, qa_s, _ = bufs
    rows = slice(c * CHUNK, (c + 1) * CHUNK)
    qt = q * ed_s[rows, :]
    qt_s[rows, :] = qt.astype(_BF16)
    qi = (qt * eh_s[c:c + 1, :]).astype(_BF16)
    for h in range(N_HEADS):
        qa_s[c * N_HEADS + h, :, 0:HEAD_DIM] = qi[:, _head(h)]


def _scores_and_states(c, mask, bufs, v_ref, st_ref):
    qt_s, kt_s, ks_s, _, eh_s, qa_s, sv_s = bufs
    rows = slice(c * CHUNK, (c + 1) * CHUNK)
    e_half = eh_s[c:c + 1, :]
    decay = e_half * e_half
    for h in range(N_HEADS):
        sl = _head(h)
        i = c * N_HEADS + h
        scores = lax.dot_general(qt_s[rows, sl], kt_s[rows, sl], _NT, preferred_element_type=_F32)
        qa_s[i, :, HEAD_DIM:HEAD_DIM + CHUNK] = jnp.where(mask, scores, 0.0).astype(_BF16)
        v = v_ref[0, rows, sl]
        upd = lax.dot_general(v, ks_s[rows, sl], _TN, preferred_element_type=_F32)
        st = st_ref[h]
        sv_s[i, 0:HEAD_DIM, :] = st.T.astype(_BF16)
        sv_s[i, HEAD_DIM:HEAD_DIM + CHUNK, :] = v
        st_ref[h] = st * decay[:, sl] + upd


def _chunk_output(c, h, bufs):
    qa_s, sv_s = bufs[-2:]
    i = c * N_HEADS + h
    return jnp.dot(qa_s[i], sv_s[i], preferred_element_type=_F32)


def _normed_input(x_ref, xn_ref, ng_ref, hb_s, st_ref):
    j = pl.program_id(1)

    @pl.when(j == 0)
    def _():
        st_ref[...] = jnp.zeros_like(st_ref)
        hb_s[0] = _rmsnorm_rows(x_ref[0], ng_ref[...]).astype(_BF16)

    slot = lax.rem(j, 2)

    def store_next():
        hb_s[1 - slot] = _rmsnorm_rows(xn_ref[0], ng_ref[...]).astype(_BF16)

    return hb_s[slot], store_next


def _bwd_kernel(x_ref, xn_ref, ng_ref, w_ref, lb_ref, tri3_ref, mask_ref, o_ref, q_ref, v_ref,
                p_s, qt_s, kt_s, ks_s, ed_s, eh_s, qa_s, sv_s, st_ref, hb_s):
    n_chunks = x_ref.shape[1] // CHUNK
    hb, store_next = _normed_input(x_ref, xn_ref, ng_ref, hb_s, st_ref)
    lb = _lower_bound(lb_ref[...])
    tri3 = tri3_ref[...]
    mask = mask_ref[...] > 0.5
    bufs = (qt_s, kt_s, ks_s, ed_s, eh_s, qa_s, sv_s)
    p_s[...] = jnp.dot(hb, w_ref[:, D_MODEL:2 * D_MODEL], preferred_element_type=_F32)
    store_next()
    qp = jnp.dot(hb, w_ref[:, 0:D_MODEL], preferred_element_type=_F32)
    q_ref[0] = qp * _sigmoid(qp)
    v_ref[0] = jnp.dot(hb, w_ref[:, 2 * D_MODEL:3 * D_MODEL], preferred_element_type=_F32).astype(_BF16)
    for c in range(n_chunks):
        _prepare_keys(c, p_s[c * CHUNK:(c + 1) * CHUNK, :], lb, tri3, 0, bufs)
    for c in range(n_chunks):
        _prepare_queries(c, q_ref[0, c * CHUNK:(c + 1) * CHUNK, :], bufs)
    for c in reversed(range(n_chunks)):
        _scores_and_states(c, mask, bufs, v_ref, st_ref)
    for c in range(n_chunks):
        rows = slice(c * CHUNK, (c + 1) * CHUNK)
        for h in range(N_HEADS):
            o_ref[0, rows, _head(h)] = _chunk_output(c, h, bufs)


def _main_kernel(x_ref, xn_ref, ob_ref, q_ref, v_ref, ng_ref, wa_ref, wb_ref, lng_ref, lnb_ref,
                 ws_ref, bs_ref, lb_ref, gn_ref, wo_ref, fg_ref, tri3_ref, mask_ref, y_ref,
                 p_s, qt_s, kt_s, ks_s, ed_s, eh_s, qa_s, sv_s, z_s, mix_s, st_ref, hb_s, pa_s):
    tile = x_ref.shape[1]
    n_chunks = tile // CHUNK
    hb, store_next = _normed_input(x_ref, xn_ref, ng_ref, hb_s, st_ref)
    lb = _lower_bound(lb_ref[...])
    tri3 = tri3_ref[...]
    mask = mask_ref[...] > 0.5
    bufs = (qt_s, kt_s, ks_s, ed_s, eh_s, qa_s, sv_s)

    p_s[...] = jnp.dot(hb, wb_ref[:, 0:D_MODEL], preferred_element_type=_F32)
    store_next()

    pa_s[...] = jnp.dot(hb, wa_ref[...], preferred_element_type=_F32)
    for c in range(n_chunks):
        _prepare_keys(c, p_s[c * CHUNK:(c + 1) * CHUNK, :], lb, tri3, CHUNK - 1, bufs)
    for n in range(tile // GMLP_CHUNK):
        rs = slice(n * GMLP_CHUNK, (n + 1) * GMLP_CHUNK)
        vv = pa_s[rs, D_MODEL:2 * D_MODEL]
        vc = vv - jnp.mean(vv, axis=-1, keepdims=True)
        vn = vc * lax.rsqrt(jnp.mean(vc * vc, axis=-1, keepdims=True) + EPS)
        vn = (vn * lng_ref[...] + lnb_ref[...]).astype(_BF16)
        for h in range(N_HEADS):
            sl = _head(h)
            s = jnp.dot(ws_ref[h], vn[:, sl], preferred_element_type=_F32) + bs_ref[:, sl]
            za = pa_s[rs, 2 * D_MODEL + h * HEAD_DIM:2 * D_MODEL + (h + 1) * HEAD_DIM]
            mix_s[rs, sl] = ((pa_s[rs, sl] * (za * _sigmoid(za))) * s).astype(_BF16)

    zb = jnp.dot(hb, wb_ref[:, D_MODEL:2 * D_MODEL], preferred_element_type=_F32)
    z_s[...] = (zb * _sigmoid(zb)) * gn_ref[...]
    for c in range(n_chunks):
        _prepare_queries(c, q_ref[0, c * CHUNK:(c + 1) * CHUNK, :], bufs)
    for c in range(n_chunks):
        _scores_and_states(c, mask, bufs, v_ref, st_ref)
    for c in range(n_chunks):
        rows = slice(c * CHUNK, (c + 1) * CHUNK)
        for h in range(N_HEADS):
            sl = _head(h)
            o = _chunk_output(c, h, bufs) + ob_ref[0, rows, sl]
            o = o * lax.rsqrt(jnp.mean(o * o, axis=-1, keepdims=True) + EPS)
            mix_s[rows, D_MODEL + h * HEAD_DIM:D_MODEL + (h + 1) * HEAD_DIM] = (
                o * z_s[rows, sl]).astype(_BF16)

    out = jnp.dot(mix_s[...], wo_ref[...], preferred_element_type=_F32)
    y_ref[0] = _rmsnorm_rows(x_ref[0] + out, fg_ref[...])


def _const_spec(shape):
    return pl.BlockSpec(shape, lambda b, j: (0,) * len(shape), pipeline_mode=pl.Buffered(1))


def _direction_constants(reverse):
    r = lax.broadcasted_iota(jnp.int32, (CHUNK, CHUNK), 0)
    c = lax.broadcasted_iota(jnp.int32, (CHUNK, CHUNK), 1)
    keep = (c >= r) if reverse else (c <= r)
    mask = keep.astype(_F32)
    tri3 = jnp.concatenate([mask, mask, mask], axis=1).astype(_BF16)
    return tri3, mask


def _recurrence_scratch(tile):
    n_hc = (tile // CHUNK) * N_HEADS
    act_bf16 = pltpu.VMEM((tile, D_MODEL), _BF16)
    return [
        pltpu.VMEM((tile, D_MODEL), _F32),
        act_bf16, act_bf16, act_bf16,
        pltpu.VMEM((tile, D_MODEL), _F32),
        pltpu.VMEM((tile // CHUNK, D_MODEL), _F32),
        pltpu.VMEM((n_hc, CHUNK, HEAD_DIM + CHUNK), _BF16),
        pltpu.VMEM((n_hc, HEAD_DIM + CHUNK, HEAD_DIM), _BF16),
    ]


def _trunk(x, p, tile=TILE, tile_bwd=TILE_BWD):
    bsz, seq, d = x.shape
    assert d == D_MODEL and seq % tile == 0 and seq % tile_bwd == 0 and tile % GMLP_CHUNK == 0
    cparams = pltpu.CompilerParams(dimension_semantics=("arbitrary", "arbitrary"),
                                   vmem_limit_bytes=VMEM_LIMIT_BYTES)
    state = pltpu.VMEM((N_HEADS, HEAD_DIM, HEAD_DIM), _F32)
    o_bwd, q_act, v_act = _bwd_sweep(x, p, tile_bwd, cparams, state)
    return _main_sweep(x, o_bwd, q_act, v_act, p, tile, cparams, state)


def _bwd_sweep(x, p, tile, cparams, state):
    bsz, seq, _ = x.shape
    nt = seq // tile
    tri3_b, mask_b = _direction_constants(True)
    rev_tile = lambda b, j: (b, nt - 1 - j, 0)
    rev_next = lambda b, j: (b, jnp.maximum(nt - 2 - j, 0), 0)
    act = lambda dtype: jax.ShapeDtypeStruct((bsz, seq, D_MODEL), dtype)
    return pl.pallas_call(
        _bwd_kernel,
        grid=(bsz, nt),
        in_specs=[
            pl.BlockSpec((1, tile, D_MODEL), rev_tile),
            pl.BlockSpec((1, tile, D_MODEL), rev_next),
            _const_spec((1, D_MODEL)),
            _const_spec((D_MODEL, 3 * D_MODEL)),
            _const_spec((2, D_MODEL)),
            _const_spec((CHUNK, 3 * CHUNK)),
            _const_spec((CHUNK, CHUNK)),
        ],
        out_specs=[pl.BlockSpec((1, tile, D_MODEL), rev_tile)] * 3,
        out_shape=[act(_F32), act(_F32), act(_BF16)],
        scratch_shapes=_recurrence_scratch(tile) + [state, pltpu.VMEM((2, tile, D_MODEL), _BF16)],
        compiler_params=cparams,
        name="hgrn_bwd_sweep",
    )(x, x, p["norm_g"], p["w_bwd"], p["lb_bwd"], tri3_b, mask_b)


def _main_sweep(x, o_bwd, q_act, v_act, p, tile, cparams, state):
    bsz, seq, _ = x.shape
    nt = seq // tile
    tri3_f, mask_f = _direction_constants(False)
    fwd_tile = lambda b, j: (b, j, 0)
    fwd_next = lambda b, j: (b, jnp.minimum(j + 1, nt - 1), 0)
    return pl.pallas_call(
        _main_kernel,
        grid=(bsz, nt),
        in_specs=[
            pl.BlockSpec((1, tile, D_MODEL), fwd_tile),
            pl.BlockSpec((1, tile, D_MODEL), fwd_next),
            pl.BlockSpec((1, tile, D_MODEL), fwd_tile),
            pl.BlockSpec((1, tile, D_MODEL), fwd_tile),
            pl.BlockSpec((1, tile, D_MODEL), fwd_tile),
            _const_spec((1, D_MODEL)),
            _const_spec((D_MODEL, 3 * D_MODEL)),
            _const_spec((D_MODEL, 2 * D_MODEL)),
            _const_spec((1, D_MODEL)),
            _const_spec((1, D_MODEL)),
            _const_spec((N_HEADS, GMLP_CHUNK, GMLP_CHUNK)),
            _const_spec((GMLP_CHUNK, D_MODEL)),
            _const_spec((2, D_MODEL)),
            _const_spec((1, D_MODEL)),
            _const_spec((2 * D_MODEL, D_MODEL)),
            _const_spec((1, D_MODEL)),
            _const_spec((CHUNK, 3 * CHUNK)),
            _const_spec((CHUNK, CHUNK)),
        ],
        out_specs=pl.BlockSpec((1, tile, D_MODEL), fwd_tile),
        out_shape=jax.ShapeDtypeStruct((bsz, seq, D_MODEL), _F32),
        scratch_shapes=_recurrence_scratch(tile) + [
            pltpu.VMEM((tile, D_MODEL), _F32),
            pltpu.VMEM((tile, 2 * D_MODEL), _BF16),
            state,
            pltpu.VMEM((2, tile, D_MODEL), _BF16),
            pltpu.VMEM((tile, 3 * D_MODEL), _F32)],
        compiler_params=cparams,
        name="encoder_main_sweep",
    )(x, x, o_bwd, q_act, v_act, p["norm_g"], p["w_a"], p["w_b"], p["ln_g"], p["ln_b"], p["w_s"], p["b_s"],
      p["lb_fwd"], p["gn_g"], p["w_out"], p["final_g"], tri3_f, mask_f)


def _prepare(norm_g, w_in, ln_v_g, ln_v_b, w_s, b_s, lb_params, gn_g, w_out, final_g):
    d = D_MODEL
    w = w_in[0].astype(_BF16)
    col = lambda i: w[:, i * d:(i + 1) * d]
    row = lambda a: a.reshape(1, -1).astype(_F32)
    return {
        "norm_g": row(norm_g[0]),
        "w_a": w[:, 0:3 * d],
        "w_b": jnp.concatenate([col(4), col(7)], axis=1),
        "w_bwd": jnp.concatenate([col(3), col(5), col(6)], axis=1),
        "ln_g": row(ln_v_g[0]),
        "ln_b": row(ln_v_b[0]),
        "w_s": w_s[0].astype(_BF16),
        "b_s": jnp.repeat(b_s[0].T.astype(_F32), HEAD_DIM, axis=1),
        "lb_fwd": lb_params[0, :, :].astype(_F32),
        "lb_bwd": lb_params[1, :, :].astype(_F32),
        "gn_g": row(gn_g[0]),
        "w_out": w_out[0].astype(_BF16),
        "final_g": row(final_g),
    }


def kernel(x_prompt, x_sample, norm_g, w_in, ln_v_g, ln_v_b, w_s, b_s, lb_params, gn_g, w_out, final_g):
    p = _prepare(norm_g, w_in, ln_v_g, ln_v_b, w_s, b_s, lb_params, gn_g, w_out, final_g)
    return (_trunk(x_prompt, p), _trunk(x_sample, p))
```

```python
import jax
import jax.numpy as jnp
from jax import lax
from jax.experimental import pallas as pl
from jax.experimental.pallas import tpu as pltpu

D_MODEL = 1024
N_HEADS = 8
HEAD_DIM = 128
GMLP_CHUNK = 128
CHUNK = 64
EPS = 1e-6
TILE = 256
TILE_BWD = 256
VMEM_LIMIT_BYTES = 56 * 1024 * 1024

_BF16 = jnp.bfloat16
_F32 = jnp.float32
_LOG2E = 1.4426950408889634
_NT = (((1,), (1,)), ((), ()))
_TN = (((0,), (0,)), ((), ()))


def _head(h):
    return slice(h * HEAD_DIM, (h + 1) * HEAD_DIM)


def _exp(x, sign=1.0):
    return jnp.exp2(x * (sign * _LOG2E))


def _sigmoid(x):
    return 1.0 / (1.0 + _exp(x, -1.0))


def _rmsnorm_rows(x, gain):
    ms = jnp.mean(x * x, axis=-1, keepdims=True)
    return (x * lax.rsqrt(ms + EPS)) * gain


def _lower_bound(lb_pair):
    m = jnp.max(lb_pair, axis=0, keepdims=True)
    e = jnp.exp(lb_pair - m)
    return e[0:1, :] / jnp.sum(e, axis=0, keepdims=True)


def _cumsum_rows(g, tri3):
    hi = g.astype(_BF16)
    r1 = g - hi.astype(_F32)
    mid = r1.astype(_BF16)
    lo = (r1 - mid.astype(_F32)).astype(_BF16)
    g3 = jnp.concatenate([hi, mid, lo], axis=0)
    return jnp.dot(tri3, g3, preferred_element_type=_F32)


def _prepare_keys(c, f_ref, lb, tri3, end_row, bufs):
    _, kt_s, ks_s, ed_s, eh_s, _, _ = bufs
    rows = slice(c * CHUNK, (c + 1) * CHUNK)
    fg = lb + (1.0 - lb) * _sigmoid(f_ref[rows, :])
    g = jnp.log(fg)
    k = 1.0 - fg
    b = _cumsum_rows(g, tri3)
    half = 0.5 * b[end_row:end_row + 1, :]
    d = b - half
    e_half = _exp(half)
    kt = k * _exp(d, -1.0)
    kt_s[rows, :] = kt.astype(_BF16)
    ks_s[rows, :] = (kt * e_half).astype(_BF16)
    ed_s[rows, :] = _exp(d)
    eh_s[c:c + 1, :] = e_half


def _prepare_queries(c, q_ref, bufs):
    qt_s, _, _, ed_s, eh_s, qa_s, _ = bufs
    rows = slice(c * CHUNK, (c + 1) * CHUNK)
    qt = q_ref[0, rows, :].astype(_F32) * ed_s[rows, :]
    qt_s[rows, :] = qt.astype(_BF16)
    qi = (qt * eh_s[c:c + 1, :]).astype(_BF16)
    for h in range(N_HEADS):
        qa_s[c * N_HEADS + h, :, 0:HEAD_DIM] = qi[:, _head(h)]


def _scores_and_states(c, mask, bufs, v_ref, st_ref):
    qt_s, kt_s, ks_s, _, eh_s, qa_s, sv_s = bufs
    rows = slice(c * CHUNK, (c + 1) * CHUNK)
    e_half = eh_s[c:c + 1, :]
    decay = e_half * e_half
    for h in range(N_HEADS):
        sl = _head(h)
        i = c * N_HEADS + h
        scores = lax.dot_general(qt_s[rows, sl], kt_s[rows, sl], _NT, preferred_element_type=_F32)
        qa_s[i, :, HEAD_DIM:HEAD_DIM + CHUNK] = jnp.where(mask, scores, 0.0).astype(_BF16)
        v = v_ref[0, rows, sl]
        upd = lax.dot_general(v, ks_s[rows, sl], _TN, preferred_element_type=_F32)
        st = st_ref[h]
        sv_s[i, 0:HEAD_DIM, :] = st.T.astype(_BF16)
        sv_s[i, HEAD_DIM:HEAD_DIM + CHUNK, :] = v
        st_ref[h] = st * decay[:, sl] + upd


def _chunk_output(c, h, bufs):
    qa_s, sv_s = bufs[-2:]
    i = c * N_HEADS + h
    return jnp.dot(qa_s[i], sv_s[i], preferred_element_type=_F32)


def _normed_input(x_ref, ng_ref, st_ref):
    @pl.when(pl.program_id(1) == 0)
    def _():
        st_ref[...] = jnp.zeros_like(st_ref)

    return _rmsnorm_rows(x_ref[0], ng_ref[...]).astype(_BF16)


def _bwd_kernel(x_ref, ng_ref, w_ref, lb_ref, tri3_ref, mask_ref, o_ref, q_ref, v_ref,
                p_s, qt_s, kt_s, ks_s, ed_s, eh_s, qa_s, sv_s, st_ref):
    n_chunks = x_ref.shape[1] // CHUNK
    hb = _normed_input(x_ref, ng_ref, st_ref)
    lb = _lower_bound(lb_ref[...])
    tri3 = tri3_ref[...]
    mask = mask_ref[...] > 0.5
    bufs = (qt_s, kt_s, ks_s, ed_s, eh_s, qa_s, sv_s)
    p_s[...] = jnp.dot(hb, w_ref[:, D_MODEL:2 * D_MODEL], preferred_element_type=_F32)
    qp = jnp.dot(hb, w_ref[:, 0:D_MODEL], preferred_element_type=_F32)
    q_ref[0] = (qp * _sigmoid(qp)).astype(_BF16)
    v_ref[0] = jnp.dot(hb, w_ref[:, 2 * D_MODEL:3 * D_MODEL], preferred_element_type=_F32).astype(_BF16)
    for c in range(n_chunks):
        _prepare_keys(c, p_s, lb, tri3, 0, bufs)
    for c in range(n_chunks):
        _prepare_queries(c, q_ref, bufs)
    for c in reversed(range(n_chunks)):
        _scores_and_states(c, mask, bufs, v_ref, st_ref)
    for c in range(n_chunks):
        rows = slice(c * CHUNK, (c + 1) * CHUNK)
        for h in range(N_HEADS):
            o_ref[0, rows, _head(h)] = _chunk_output(c, h, bufs).astype(_BF16)


def _main_kernel(x_ref, ob_ref, q_ref, v_ref, ng_ref, wa_ref, wb_ref, lng_ref, lnb_ref,
                 ws_ref, bs_ref, lb_ref, gn_ref, wo_ref, fg_ref, tri3_ref, mask_ref, y_ref,
                 p_s, qt_s, kt_s, ks_s, ed_s, eh_s, qa_s, sv_s, z_s, mix_s, st_ref, pa_s):
    tile = x_ref.shape[1]
    n_chunks = tile // CHUNK
    hb = _normed_input(x_ref, ng_ref, st_ref)
    lb = _lower_bound(lb_ref[...])
    tri3 = tri3_ref[...]
    mask = mask_ref[...] > 0.5
    bufs = (qt_s, kt_s, ks_s, ed_s, eh_s, qa_s, sv_s)

    p_s[...] = jnp.dot(hb, wb_ref[:, 0:D_MODEL], preferred_element_type=_F32)

    pa_s[...] = jnp.dot(hb, wa_ref[...], preferred_element_type=_F32)
    for c in range(n_chunks):
        _prepare_keys(c, p_s, lb, tri3, CHUNK - 1, bufs)
    for n in range(tile // GMLP_CHUNK):
        rs = slice(n * GMLP_CHUNK, (n + 1) * GMLP_CHUNK)
        vv = pa_s[rs, D_MODEL:2 * D_MODEL]
        vc = vv - jnp.mean(vv, axis=-1, keepdims=True)
        vn = vc * lax.rsqrt(jnp.mean(vc * vc, axis=-1, keepdims=True) + EPS)
        vn = (vn * lng_ref[...] + lnb_ref[...]).astype(_BF16)
        for h in range(N_HEADS):
            sl = _head(h)
            s = jnp.dot(ws_ref[h], vn[:, sl], preferred_element_type=_F32) + bs_ref[:, sl]
            za = pa_s[rs, 2 * D_MODEL + h * HEAD_DIM:2 * D_MODEL + (h + 1) * HEAD_DIM]
            mix_s[rs, sl] = ((pa_s[rs, sl] * (za * _sigmoid(za))) * s).astype(_BF16)

    zb = jnp.dot(hb, wb_ref[:, D_MODEL:2 * D_MODEL], preferred_element_type=_F32)
    z_s[...] = (zb * _sigmoid(zb)) * gn_ref[...]
    for c in range(n_chunks):
        _prepare_queries(c, q_ref, bufs)
    for c in range(n_chunks):
        _scores_and_states(c, mask, bufs, v_ref, st_ref)
    for c in range(n_chunks):
        rows = slice(c * CHUNK, (c + 1) * CHUNK)
        for h in range(N_HEADS):
            sl = _head(h)
            o = _chunk_output(c, h, bufs) + ob_ref[0, rows, sl].astype(_F32)
            o = o * lax.rsqrt(jnp.mean(o * o, axis=-1, keepdims=True) + EPS)
            mix_s[rows, D_MODEL + h * HEAD_DIM:D_MODEL + (h + 1) * HEAD_DIM] = (
                o * z_s[rows, sl]).astype(_BF16)

    out = jnp.dot(mix_s[...], wo_ref[...], preferred_element_type=_F32)
    y_ref[0] = _rmsnorm_rows(x_ref[0] + out, fg_ref[...])


def _const_spec(shape):
    return pl.BlockSpec(shape, lambda b, j: (0,) * len(shape), pipeline_mode=pl.Buffered(1))


def _direction_constants(reverse):
    r = lax.broadcasted_iota(jnp.int32, (CHUNK, CHUNK), 0)
    c = lax.broadcasted_iota(jnp.int32, (CHUNK, CHUNK), 1)
    keep = (c >= r) if reverse else (c <= r)
    mask = keep.astype(_F32)
    tri3 = jnp.concatenate([mask, mask, mask], axis=1).astype(_BF16)
    return tri3, mask


def _recurrence_scratch(tile):
    n_hc = (tile // CHUNK) * N_HEADS
    act_bf16 = pltpu.VMEM((tile, D_MODEL), _BF16)
    return [
        pltpu.VMEM((tile, D_MODEL), _F32),
        act_bf16, act_bf16, act_bf16,
        pltpu.VMEM((tile, D_MODEL), _F32),
        pltpu.VMEM((tile // CHUNK, D_MODEL), _F32),
        pltpu.VMEM((n_hc, CHUNK, HEAD_DIM + CHUNK), _BF16),
        pltpu.VMEM((n_hc, HEAD_DIM + CHUNK, HEAD_DIM), _BF16),
    ]


def _trunk(x, p, tile=TILE, tile_bwd=TILE_BWD):
    bsz, seq, d = x.shape
    assert d == D_MODEL and seq % tile == 0 and seq % tile_bwd == 0 and tile % GMLP_CHUNK == 0
    cparams = pltpu.CompilerParams(dimension_semantics=("arbitrary", "arbitrary"),
                                   vmem_limit_bytes=VMEM_LIMIT_BYTES)
    state = pltpu.VMEM((N_HEADS, HEAD_DIM, HEAD_DIM), _F32)
    o_bwd, q_act, v_act = _bwd_sweep(x, p, tile_bwd, cparams, state)
    return _main_sweep(x, o_bwd, q_act, v_act, p, tile, cparams, state)


def _bwd_sweep(x, p, tile, cparams, state):
    bsz, seq, _ = x.shape
    nt = seq // tile
    tri3_b, mask_b = _direction_constants(True)
    rev_tile = lambda b, j: (b, nt - 1 - j, 0)
    act = lambda dtype: jax.ShapeDtypeStruct((bsz, seq, D_MODEL), dtype)
    return pl.pallas_call(
        _bwd_kernel,
        grid=(bsz, nt),
        in_specs=[
            pl.BlockSpec((1, tile, D_MODEL), rev_tile),
            _const_spec((1, D_MODEL)),
            _const_spec((D_MODEL, 3 * D_MODEL)),
            _const_spec((2, D_MODEL)),
            _const_spec((CHUNK, 3 * CHUNK)),
            _const_spec((CHUNK, CHUNK)),
        ],
        out_specs=[pl.BlockSpec((1, tile, D_MODEL), rev_tile)] * 3,
        out_shape=[act(_BF16), act(_BF16), act(_BF16)],
        scratch_shapes=_recurrence_scratch(tile) + [state],
        compiler_params=cparams,
        name="hgrn_bwd_sweep",
    )(x, p["norm_g"], p["w_bwd"], p["lb_bwd"], tri3_b, mask_b)


def _main_sweep(x, o_bwd, q_act, v_act, p, tile, cparams, state):
    bsz, seq, _ = x.shape
    nt = seq // tile
    tri3_f, mask_f = _direction_constants(False)
    fwd_tile = lambda b, j: (b, j, 0)
    return pl.pallas_call(
        _main_kernel,
        grid=(bsz, nt),
        in_specs=[
            pl.BlockSpec((1, tile, D_MODEL), fwd_tile),
            pl.BlockSpec((1, tile, D_MODEL), fwd_tile),
            pl.BlockSpec((1, tile, D_MODEL), fwd_tile),
            pl.BlockSpec((1, tile, D_MODEL), fwd_tile),
            _const_spec((1, D_MODEL)),
            _const_spec((D_MODEL, 3 * D_MODEL)),
            _const_spec((D_MODEL, 2 * D_MODEL)),
            _const_spec((1, D_MODEL)),
            _const_spec((1, D_MODEL)),
            _const_spec((N_HEADS, GMLP_CHUNK, GMLP_CHUNK)),
            _const_spec((GMLP_CHUNK, D_MODEL)),
            _const_spec((2, D_MODEL)),
            _const_spec((1, D_MODEL)),
            _const_spec((2 * D_MODEL, D_MODEL)),
            _const_spec((1, D_MODEL)),
            _const_spec((CHUNK, 3 * CHUNK)),
            _const_spec((CHUNK, CHUNK)),
        ],
        out_specs=pl.BlockSpec((1, tile, D_MODEL), fwd_tile),
        out_shape=jax.ShapeDtypeStruct((bsz, seq, D_MODEL), _F32),
        scratch_shapes=_recurrence_scratch(tile) + [
            pltpu.VMEM((tile, D_MODEL), _F32),
            pltpu.VMEM((tile, 2 * D_MODEL), _BF16),
            state,
            pltpu.VMEM((tile, 3 * D_MODEL), _F32)],
        compiler_params=cparams,
        name="encoder_main_sweep",
    )(x, o_bwd, q_act, v_act, p["norm_g"], p["w_a"], p["w_b"], p["ln_g"], p["ln_b"], p["w_s"], p["b_s"],
      p["lb_fwd"], p["gn_g"], p["w_out"], p["final_g"], tri3_f, mask_f)


def _prepare(norm_g, w_in, ln_v_g, ln_v_b, w_s, b_s, lb_params, gn_g, w_out, final_g):
    d = D_MODEL
    w = w_in[0].astype(_BF16)
    col = lambda i: w[:, i * d:(i + 1) * d]
    row = lambda a: a.reshape(1, -1).astype(_F32)
    return {
        "norm_g": row(norm_g[0]),
        "w_a": w[:, 0:3 * d],
        "w_b": jnp.concatenate([col(4), col(7)], axis=1),
        "w_bwd": jnp.concatenate([col(3), col(5), col(6)], axis=1),
        "ln_g": row(ln_v_g[0]),
        "ln_b": row(ln_v_b[0]),
        "w_s": w_s[0].astype(_BF16),
        "b_s": jnp.repeat(b_s[0].T.astype(_F32), HEAD_DIM, axis=1),
        "lb_fwd": lb_params[0, :, :].astype(_F32),
        "lb_bwd": lb_params[1, :, :].astype(_F32),
        "gn_g": row(gn_g[0]),
        "w_out": w_out[0].astype(_BF16),
        "final_g": row(final_g),
    }


def kernel(x_prompt, x_sample, norm_g, w_in, ln_v_g, ln_v_b, w_s, b_s, lb_params, gn_g, w_out, final_g):
    p = _prepare(norm_g, w_in, ln_v_g, ln_v_b, w_s, b_s, lb_params, gn_g, w_out, final_g)
    return (_trunk(x_prompt, p), _trunk(x_sample, p))
```

```python
import functools

import jax
import jax.numpy as jnp
from jax import lax
from jax.experimental import pallas as pl
from jax.experimental.pallas import tpu as pltpu

D_MODEL = 1024
N_HEADS = 8
HEAD_DIM = 128
GMLP_CHUNK = 128
CHUNK = 64
EPS = 1e-6
TILE = 256
TILE_BWD = 256
VMEM_LIMIT_BYTES = 56 * 1024 * 1024

_BF16 = jnp.bfloat16
_F32 = jnp.float32
_LOG2E = 1.4426950408889634
_HALF_DECAY_FLOOR = 1e-26
_NT = (((1,), (1,)), ((), ()))
_TN = (((0,), (0,)), ((), ()))


def _head(h):
    return slice(h * HEAD_DIM, (h + 1) * HEAD_DIM)


def _exp(x, sign=1.0):
    return jnp.exp2(x * (sign * _LOG2E))


def _sigmoid(x):
    return 1.0 / (1.0 + _exp(x, -1.0))


def _rmsnorm_rows(x, gain):
    ms = jnp.mean(x * x, axis=-1, keepdims=True)
    return (x * lax.rsqrt(ms + EPS)) * gain


def _lower_bound(lb_pair):
    m = jnp.max(lb_pair, axis=0, keepdims=True)
    e = jnp.exp(lb_pair - m)
    return e[0:1, :] / jnp.sum(e, axis=0, keepdims=True)


def _cumsum_rows(g, tri2):
    hi = g.astype(_BF16)
    lo = (g - hi.astype(_F32)).astype(_BF16)
    g2 = jnp.concatenate([hi, lo], axis=0)
    return jnp.dot(tri2, g2, preferred_element_type=_F32)


def _prepare_keys(c, f_ref, lb, tri2, end_row, bufs):
    _, kt_s, ks_s, ed_s, eh_s, _, _ = bufs
    rows = slice(c * CHUNK, (c + 1) * CHUNK)
    fg = lb + (1.0 - lb) * _sigmoid(f_ref[rows, :])
    g = jnp.log(fg)
    k = 1.0 - fg
    b = _cumsum_rows(g, tri2)
    half = 0.5 * b[end_row:end_row + 1, :]
    d = b - half
    e_half = _exp(half)
    kt = k * _exp(d, -1.0)
    kt_s[rows, :] = kt.astype(_BF16)
    ks_s[rows, :] = (kt * e_half).astype(_BF16)
    ed_s[rows, :] = _exp(d)
    eh_s[c:c + 1, :] = e_half


def _prepare_queries(c, q_ref, bufs):
    qt_s, _, _, ed_s, eh_s, qa_s, _ = bufs
    rows = slice(c * CHUNK, (c + 1) * CHUNK)
    qt = q_ref[0, rows, :].astype(_F32) * ed_s[rows, :]
    qt_s[rows, :] = qt.astype(_BF16)
    qi = (qt * eh_s[c:c + 1, :]).astype(_BF16)
    for h in range(N_HEADS):
        qa_s[c * N_HEADS + h, :, 0:HEAD_DIM] = qi[:, _head(h)]


def _general_chunk(c, f_ref, q_ref, lb, tri2, mask, end_row, bufs):
    _, _, ks_s, ed_s, _, qa_s, _ = bufs
    r0 = c * CHUNK
    rows = slice(r0, r0 + CHUNK)
    fg = lb + (1.0 - lb) * _sigmoid(f_ref[rows, :])
    g = jnp.log(fg)
    k = 1.0 - fg
    b = _cumsum_rows(g, tri2)
    total = b[end_row:end_row + 1, :]
    q = q_ref[0, rows, :].astype(_F32)
    ks_s[rows, :] = (k * _exp(total - b)).astype(_BF16)
    qi = (q * _exp(b)).astype(_BF16)
    for h in range(N_HEADS):
        qa_s[c * N_HEADS + h, :, 0:HEAD_DIM] = qi[:, _head(h)]
    ed_s[rows, :] = b
    f_ref[rows, :] = k
    col = lax.broadcasted_iota(jnp.int32, (CHUNK, CHUNK), 1)

    def one_key_row(s, acc):
        b_s = ed_s[pl.ds(r0 + s, 1), :]
        k_s = f_ref[pl.ds(r0 + s, 1), :]
        w = q * _exp(jnp.minimum(b - b_s, 0.0)) * k_s
        return tuple(a + jnp.where(col == s, jnp.sum(w[:, _head(h)], axis=-1, keepdims=True), 0.0)
                     for h, a in enumerate(acc))

    zeros = tuple(jnp.zeros((CHUNK, CHUNK), _F32) for _ in range(N_HEADS))
    acc = lax.fori_loop(0, CHUNK, one_key_row, zeros)
    for h in range(N_HEADS):
        qa_s[c * N_HEADS + h, :, HEAD_DIM:HEAD_DIM + CHUNK] = jnp.where(mask, acc[h], 0.0).astype(_BF16)


def _recurrence(general, chunk_order, f_ref, q_ref, v_ref, st_ref, lb, tri2, mask, end_row, bufs, emit,
                range_ref):
    eh_s = bufs[4]
    range_ref[0, 0] = jnp.broadcast_to(jnp.min(eh_s[...], axis=0, keepdims=True), range_ref.shape[2:])
    for c in sorted(chunk_order):
        if general:
            _general_chunk(c, f_ref, q_ref, lb, tri2, mask, end_row, bufs)
        else:
            _prepare_queries(c, q_ref, bufs)
    for c in chunk_order:
        _state_step(c, bufs, v_ref, st_ref, split_scores_mask=None if general else mask)
    for c in sorted(chunk_order):
        for h in range(N_HEADS):
            emit(c, h, _chunk_output(c, h, bufs))


def _state_step(c, bufs, v_ref, st_ref, split_scores_mask=None):
    qt_s, kt_s, ks_s, _, eh_s, qa_s, sv_s = bufs
    rows = slice(c * CHUNK, (c + 1) * CHUNK)
    e_half = eh_s[c:c + 1, :]
    decay = e_half * e_half
    for h in range(N_HEADS):
        sl = _head(h)
        i = c * N_HEADS + h
        if split_scores_mask is not None:
            scores = lax.dot_general(qt_s[rows, sl], kt_s[rows, sl], _NT, preferred_element_type=_F32)
            qa_s[i, :, HEAD_DIM:HEAD_DIM + CHUNK] = jnp.where(split_scores_mask, scores, 0.0).astype(_BF16)
        v = v_ref[0, rows, sl]
        upd = lax.dot_general(v, ks_s[rows, sl], _TN, preferred_element_type=_F32)
        st = st_ref[h]
        sv_s[i, 0:HEAD_DIM, :] = st.T.astype(_BF16)
        sv_s[i, HEAD_DIM:HEAD_DIM + CHUNK, :] = v
        st_ref[h] = st * decay[:, sl] + upd


def _chunk_output(c, h, bufs):
    qa_s, sv_s = bufs[-2:]
    i = c * N_HEADS + h
    return jnp.dot(qa_s[i], sv_s[i], preferred_element_type=_F32)


def _normed_input(x_ref, ng_ref, st_ref):
    @pl.when(pl.program_id(1) == 0)
    def _():
        st_ref[...] = jnp.zeros_like(st_ref)

    return _rmsnorm_rows(x_ref[0], ng_ref[...]).astype(_BF16)


def _bwd_kernel(general, x_ref, ng_ref, w_ref, lb_ref, tri2_ref, mask_ref, o_ref, q_ref, v_ref, range_ref,
                p_s, qt_s, kt_s, ks_s, ed_s, eh_s, qa_s, sv_s, st_ref):
    n_chunks = x_ref.shape[1] // CHUNK
    hb = _normed_input(x_ref, ng_ref, st_ref)
    lb = _lower_bound(lb_ref[...])
    tri2 = tri2_ref[...]
    mask = mask_ref[...] > 0.5
    bufs = (qt_s, kt_s, ks_s, ed_s, eh_s, qa_s, sv_s)
    p_s[...] = jnp.dot(hb, w_ref[:, D_MODEL:2 * D_MODEL], preferred_element_type=_F32)
    qp = jnp.dot(hb, w_ref[:, 0:D_MODEL], preferred_element_type=_F32)
    q_ref[0] = (qp * _sigmoid(qp)).astype(_BF16)
    v_ref[0] = jnp.dot(hb, w_ref[:, 2 * D_MODEL:3 * D_MODEL], preferred_element_type=_F32).astype(_BF16)
    for c in range(n_chunks):
        _prepare_keys(c, p_s, lb, tri2, 0, bufs)

    def emit(c, h, o):
        o_ref[0, c * CHUNK:(c + 1) * CHUNK, _head(h)] = o.astype(_BF16)

    _recurrence(general, list(reversed(range(n_chunks))), p_s, q_ref, v_ref, st_ref, lb, tri2, mask, 0,
                bufs, emit, range_ref)


def _main_kernel(general, x_ref, ob_ref, q_ref, v_ref, ng_ref, wa_ref, wb_ref, lng_ref, lnb_ref,
                 ws_ref, bs_ref, lb_ref, gn_ref, wo_ref, fg_ref, tri2_ref, mask_ref, y_ref, range_ref,
                 p_s, qt_s, kt_s, ks_s, ed_s, eh_s, qa_s, sv_s, z_s, mix_s, st_ref, pa_s):
    tile = x_ref.shape[1]
    n_chunks = tile // CHUNK
    hb = _normed_input(x_ref, ng_ref, st_ref)
    lb = _lower_bound(lb_ref[...])
    tri2 = tri2_ref[...]
    mask = mask_ref[...] > 0.5
    bufs = (qt_s, kt_s, ks_s, ed_s, eh_s, qa_s, sv_s)

    p_s[...] = jnp.dot(hb, wb_ref[:, 0:D_MODEL], preferred_element_type=_F32)

    pa_s[...] = jnp.dot(hb, wa_ref[...], preferred_element_type=_F32)
    for c in range(n_chunks):
        _prepare_keys(c, p_s, lb, tri2, CHUNK - 1, bufs)
    for n in range(tile // GMLP_CHUNK):
        rs = slice(n * GMLP_CHUNK, (n + 1) * GMLP_CHUNK)
        vv = pa_s[rs, D_MODEL:2 * D_MODEL]
        vc = vv - jnp.mean(vv, axis=-1, keepdims=True)
        vn = vc * lax.rsqrt(jnp.mean(vc * vc, axis=-1, keepdims=True) + EPS)
        vn = (vn * lng_ref[...] + lnb_ref[...]).astype(_BF16)
        for h in range(N_HEADS):
            sl = _head(h)
            s = jnp.dot(ws_ref[h], vn[:, sl], preferred_element_type=_F32) + bs_ref[:, sl]
            za = pa_s[rs, 2 * D_MODEL + h * HEAD_DIM:2 * D_MODEL + (h + 1) * HEAD_DIM]
            mix_s[rs, sl] = ((pa_s[rs, sl] * (za * _sigmoid(za))) * s).astype(_BF16)

    zb = jnp.dot(hb, wb_ref[:, D_MODEL:2 * D_MODEL], preferred_element_type=_F32)
    z_s[...] = (zb * _sigmoid(zb)) * gn_ref[...]

    def emit(c, h, o):
        rows, sl = slice(c * CHUNK, (c + 1) * CHUNK), _head(h)
        o = o + ob_ref[0, rows, sl].astype(_F32)
        o = o * lax.rsqrt(jnp.mean(o * o, axis=-1, keepdims=True) + EPS)
        mix_s[rows, D_MODEL + h * HEAD_DIM:D_MODEL + (h + 1) * HEAD_DIM] = (
            o * z_s[rows, sl]).astype(_BF16)

    _recurrence(general, list(range(n_chunks)), p_s, q_ref, v_ref, st_ref, lb, tri2, mask, CHUNK - 1,
                bufs, emit, range_ref)

    out = jnp.dot(mix_s[...], wo_ref[...], preferred_element_type=_F32)
    y_ref[0] = _rmsnorm_rows(x_ref[0] + out, fg_ref[...])


def _const_spec(shape):
    return pl.BlockSpec(shape, lambda b, j: (0,) * len(shape), pipeline_mode=pl.Buffered(1))


def _direction_constants(reverse):
    r = lax.broadcasted_iota(jnp.int32, (CHUNK, CHUNK), 0)
    c = lax.broadcasted_iota(jnp.int32, (CHUNK, CHUNK), 1)
    keep = (c >= r) if reverse else (c <= r)
    mask = keep.astype(_F32)
    tri2 = jnp.concatenate([mask, mask], axis=1).astype(_BF16)
    return tri2, mask


def _recurrence_scratch(tile):
    n_hc = (tile // CHUNK) * N_HEADS
    act_bf16 = pltpu.VMEM((tile, D_MODEL), _BF16)
    return [
        pltpu.VMEM((tile, D_MODEL), _F32),
        act_bf16, act_bf16, act_bf16,
        pltpu.VMEM((tile, D_MODEL), _F32),
        pltpu.VMEM((tile // CHUNK, D_MODEL), _F32),
        pltpu.VMEM((n_hc, CHUNK, HEAD_DIM + CHUNK), _BF16),
        pltpu.VMEM((n_hc, HEAD_DIM + CHUNK, HEAD_DIM), _BF16),
    ]


def _trunk(x, p, tile=TILE, tile_bwd=TILE_BWD):
    y, in_range = _sweeps(x, p, tile, tile_bwd, general=False)
    return lax.cond(in_range, lambda: y, lambda: _sweeps(x, p, tile, tile_bwd, general=True)[0])


def _sweeps(x, p, tile, tile_bwd, general):
    bsz, seq, d = x.shape
    assert d == D_MODEL and seq % tile == 0 and seq % tile_bwd == 0 and tile % GMLP_CHUNK == 0
    cparams = pltpu.CompilerParams(dimension_semantics=("arbitrary", "arbitrary"),
                                   vmem_limit_bytes=VMEM_LIMIT_BYTES)
    state = pltpu.VMEM((N_HEADS, HEAD_DIM, HEAD_DIM), _F32)
    o_bwd, q_act, v_act, range_b = _bwd_sweep(general, x, p, tile_bwd, cparams, state)
    y, range_f = _main_sweep(general, x, o_bwd, q_act, v_act, p, tile, cparams, state)
    in_range = jnp.minimum(jnp.min(range_b), jnp.min(range_f)) >= _HALF_DECAY_FLOOR
    return y, in_range


def _range_out(bsz, nt):
    return (pl.BlockSpec((1, 1, 8, D_MODEL), lambda b, j: (b, j, 0, 0)),
            jax.ShapeDtypeStruct((bsz, nt, 8, D_MODEL), _F32))


def _bwd_sweep(general, x, p, tile, cparams, state):
    bsz, seq, _ = x.shape
    nt = seq // tile
    tri2_b, mask_b = _direction_constants(True)
    rev_tile = lambda b, j: (b, nt - 1 - j, 0)
    act = lambda dtype: jax.ShapeDtypeStruct((bsz, seq, D_MODEL), dtype)
    range_spec, range_shape = _range_out(bsz, nt)
    return pl.pallas_call(
        functools.partial(_bwd_kernel, general),
        grid=(bsz, nt),
        in_specs=[
            pl.BlockSpec((1, tile, D_MODEL), rev_tile),
            _const_spec((1, D_MODEL)),
            _const_spec((D_MODEL, 3 * D_MODEL)),
            _const_spec((2, D_MODEL)),
            _const_spec((CHUNK, 2 * CHUNK)),
            _const_spec((CHUNK, CHUNK)),
        ],
        out_specs=[pl.BlockSpec((1, tile, D_MODEL), rev_tile)] * 3 + [range_spec],
        out_shape=[act(_BF16), act(_BF16), act(_BF16), range_shape],
        scratch_shapes=_recurrence_scratch(tile) + [state],
        compiler_params=cparams,
        name="hgrn_bwd_sweep_general" if general else "hgrn_bwd_sweep",
    )(x, p["norm_g"], p["w_bwd"], p["lb_bwd"], tri2_b, mask_b)


def _main_sweep(general, x, o_bwd, q_act, v_act, p, tile, cparams, state):
    bsz, seq, _ = x.shape
    nt = seq // tile
    tri2_f, mask_f = _direction_constants(False)
    fwd_tile = lambda b, j: (b, j, 0)
    range_spec, range_shape = _range_out(bsz, nt)
    return pl.pallas_call(
        functools.partial(_main_kernel, general),
        grid=(bsz, nt),
        in_specs=[
            pl.BlockSpec((1, tile, D_MODEL), fwd_tile),
            pl.BlockSpec((1, tile, D_MODEL), fwd_tile),
            pl.BlockSpec((1, tile, D_MODEL), fwd_tile),
            pl.BlockSpec((1, tile, D_MODEL), fwd_tile),
            _const_spec((1, D_MODEL)),
            _const_spec((D_MODEL, 3 * D_MODEL)),
            _const_spec((D_MODEL, 2 * D_MODEL)),
            _const_spec((1, D_MODEL)),
            _const_spec((1, D_MODEL)),
            _const_spec((N_HEADS, GMLP_CHUNK, GMLP_CHUNK)),
            _const_spec((GMLP_CHUNK, D_MODEL)),
            _const_spec((2, D_MODEL)),
            _const_spec((1, D_MODEL)),
            _const_spec((2 * D_MODEL, D_MODEL)),
            _const_spec((1, D_MODEL)),
            _const_spec((CHUNK, 2 * CHUNK)),
            _const_spec((CHUNK, CHUNK)),
        ],
        out_specs=[pl.BlockSpec((1, tile, D_MODEL), fwd_tile), range_spec],
        out_shape=[jax.ShapeDtypeStruct((bsz, seq, D_MODEL), _F32), range_shape],
        scratch_shapes=_recurrence_scratch(tile) + [
            pltpu.VMEM((tile, D_MODEL), _F32),
            pltpu.VMEM((tile, 2 * D_MODEL), _BF16),
            state,
            pltpu.VMEM((tile, 3 * D_MODEL), _F32)],
        compiler_params=cparams,
        name="encoder_main_sweep_general" if general else "encoder_main_sweep",
    )(x, o_bwd, q_act, v_act, p["norm_g"], p["w_a"], p["w_b"], p["ln_g"], p["ln_b"], p["w_s"], p["b_s"],
      p["lb_fwd"], p["gn_g"], p["w_out"], p["final_g"], tri2_f, mask_f)


def _prepare(norm_g, w_in, ln_v_g, ln_v_b, w_s, b_s, lb_params, gn_g, w_out, final_g):
    d = D_MODEL
    w = w_in[0].astype(_BF16)
    col = lambda i: w[:, i * d:(i + 1) * d]
    row = lambda a: a.reshape(1, -1).astype(_F32)
    return {
        "norm_g": row(norm_g[0]),
        "w_a": w[:, 0:3 * d],
        "w_b": jnp.concatenate([col(4), col(7)], axis=1),
        "w_bwd": jnp.concatenate([col(3), col(5), col(6)], axis=1),
        "ln_g": row(ln_v_g[0]),
        "ln_b": row(ln_v_b[0]),
        "w_s": w_s[0].astype(_BF16),
        "b_s": jnp.repeat(b_s[0].T.astype(_F32), HEAD_DIM, axis=1),
        "lb_fwd": lb_params[0, :, :].astype(_F32),
        "lb_bwd": lb_params[1, :, :].astype(_F32),
        "gn_g": row(gn_g[0]),
        "w_out": w_out[0].astype(_BF16),
        "final_g": row(final_g),
    }


def kernel(x_prompt, x_sample, norm_g, w_in, ln_v_g, ln_v_b, w_s, b_s, lb_params, gn_g, w_out, final_g):
    p = _prepare(norm_g, w_in, ln_v_g, ln_v_b, w_s, b_s, lb_params, gn_g, w_out, final_g)
    return (_trunk(x_prompt, p), _trunk(x_sample, p))
```

```python
import functools

import jax
import jax.numpy as jnp
import numpy as np
from jax import lax
from jax.experimental import pallas as pl
from jax.experimental.pallas import tpu as pltpu

D_MODEL = 1024
N_HEADS = 8
HEAD_DIM = 128
GMLP_CHUNK = 128
CHUNK = 64
EPS = 1e-6
TILE = 256
TILE_BWD = 256
VMEM_LIMIT_BYTES = 56 * 1024 * 1024

_BF16 = jnp.bfloat16
_F32 = jnp.float32
_LOG2E = 1.4426950408889634
_HALF_DECAY_FLOOR = 1e-26
_NT = (((1,), (1,)), ((), ()))
_TN = (((0,), (0,)), ((), ()))


def _head(h):
    return slice(h * HEAD_DIM, (h + 1) * HEAD_DIM)


def _exp(x, sign=1.0):
    return jnp.exp2(x * (sign * _LOG2E))


def _sigmoid(x):
    return 1.0 / (1.0 + _exp(x, -1.0))


def _rmsnorm_rows(x, gain):
    ms = jnp.mean(x * x, axis=-1, keepdims=True)
    return (x * lax.rsqrt(ms + EPS)) * gain


def _lower_bound(lb_pair):
    m = jnp.max(lb_pair, axis=0, keepdims=True)
    e = jnp.exp(lb_pair - m)
    return e[0:1, :] / jnp.sum(e, axis=0, keepdims=True)


def _cumsum_rows(g, tri2):
    hi = g.astype(_BF16)
    lo = (g - hi.astype(_F32)).astype(_BF16)
    g2 = jnp.concatenate([hi, lo], axis=0)
    return jnp.dot(tri2, g2, preferred_element_type=_F32)


def _prepare_keys(c, f_ref, lb, tri2, end_row, bufs):
    _, kt_s, ks_s, ed_s, eh_s, _, _ = bufs
    rows = slice(c * CHUNK, (c + 1) * CHUNK)
    fg = lb + (1.0 - lb) * _sigmoid(f_ref[rows, :])
    g = jnp.log(fg)
    k = 1.0 - fg
    b = _cumsum_rows(g, tri2)
    half = 0.5 * b[end_row:end_row + 1, :]
    d = b - half
    e_half = _exp(half)
    kt = k * _exp(d, -1.0)
    kt_s[rows, :] = kt.astype(_BF16)
    ks_s[rows, :] = (kt * e_half).astype(_BF16)
    ed_s[rows, :] = _exp(d)
    eh_s[c:c + 1, :] = e_half


def _prepare_queries(c, q_ref, bufs):
    qt_s, _, _, ed_s, eh_s, qa_s, _ = bufs
    rows = slice(c * CHUNK, (c + 1) * CHUNK)
    qt = q_ref[0, rows, :].astype(_F32) * ed_s[rows, :]
    qt_s[rows, :] = qt.astype(_BF16)
    qi = (qt * eh_s[c:c + 1, :]).astype(_BF16)
    for h in range(N_HEADS):
        qa_s[c * N_HEADS + h, :, 0:HEAD_DIM] = qi[:, _head(h)]


def _general_chunk(c, f_ref, q_ref, lb, tri2, mask, end_row, bufs):
    _, _, ks_s, ed_s, _, qa_s, _ = bufs
    r0 = c * CHUNK
    rows = slice(r0, r0 + CHUNK)
    fg = lb + (1.0 - lb) * _sigmoid(f_ref[rows, :])
    g = jnp.log(fg)
    k = 1.0 - fg
    b = _cumsum_rows(g, tri2)
    total = b[end_row:end_row + 1, :]
    q = q_ref[0, rows, :].astype(_F32)
    ks_s[rows, :] = (k * _exp(total - b)).astype(_BF16)
    qi = (q * _exp(b)).astype(_BF16)
    for h in range(N_HEADS):
        qa_s[c * N_HEADS + h, :, 0:HEAD_DIM] = qi[:, _head(h)]
    ed_s[rows, :] = b
    f_ref[rows, :] = k
    col = lax.broadcasted_iota(jnp.int32, (CHUNK, CHUNK), 1)

    def one_key_row(s, acc):
        b_s = ed_s[pl.ds(r0 + s, 1), :]
        k_s = f_ref[pl.ds(r0 + s, 1), :]
        w = q * _exp(jnp.minimum(b - b_s, 0.0)) * k_s
        return tuple(a + jnp.where(col == s, jnp.sum(w[:, _head(h)], axis=-1, keepdims=True), 0.0)
                     for h, a in enumerate(acc))

    zeros = tuple(jnp.zeros((CHUNK, CHUNK), _F32) for _ in range(N_HEADS))
    acc = lax.fori_loop(0, CHUNK, one_key_row, zeros)
    for h in range(N_HEADS):
        qa_s[c * N_HEADS + h, :, HEAD_DIM:HEAD_DIM + CHUNK] = jnp.where(mask, acc[h], 0.0).astype(_BF16)


def _recurrence(general, chunk_order, f_ref, q_ref, v_ref, st_ref, lb, tri2, mask, end_row, bufs, emit,
                range_ref):
    eh_s = bufs[4]
    low = jnp.min(eh_s[...], axis=0, keepdims=True)
    low = functools.reduce(jnp.minimum, [low[:, _head(h)] for h in range(N_HEADS)])
    range_ref[0, 0] = jnp.broadcast_to(low, range_ref.shape[2:])
    for c in sorted(chunk_order):
        if general:
            _general_chunk(c, f_ref, q_ref, lb, tri2, mask, end_row, bufs)
        else:
            _prepare_queries(c, q_ref, bufs)
    for c in chunk_order:
        _state_step(c, bufs, v_ref, st_ref, split_scores_mask=None if general else mask)
    for c in sorted(chunk_order):
        for h in range(N_HEADS):
            emit(c, h, _chunk_output(c, h, bufs))


def _state_step(c, bufs, v_ref, st_ref, split_scores_mask=None):
    qt_s, kt_s, ks_s, _, eh_s, qa_s, sv_s = bufs
    rows = slice(c * CHUNK, (c + 1) * CHUNK)
    e_half = eh_s[c:c + 1, :]
    decay = e_half * e_half
    for h in range(N_HEADS):
        sl = _head(h)
        i = c * N_HEADS + h
        if split_scores_mask is not None:
            scores = lax.dot_general(qt_s[rows, sl], kt_s[rows, sl], _NT, preferred_element_type=_F32)
            qa_s[i, :, HEAD_DIM:HEAD_DIM + CHUNK] = jnp.where(split_scores_mask, scores, 0.0).astype(_BF16)
        v = v_ref[0, rows, sl]
        upd = lax.dot_general(v, ks_s[rows, sl], _TN, preferred_element_type=_F32)
        st = st_ref[h]
        sv_s[i, 0:HEAD_DIM, :] = st.T.astype(_BF16)
        sv_s[i, HEAD_DIM:HEAD_DIM + CHUNK, :] = v
        st_ref[h] = st * decay[:, sl] + upd


def _chunk_output(c, h, bufs):
    qa_s, sv_s = bufs[-2:]
    i = c * N_HEADS + h
    return jnp.dot(qa_s[i], sv_s[i], preferred_element_type=_F32)


def _normed_input(x_ref, ng_ref, st_ref):
    @pl.when(pl.program_id(1) == 0)
    def _():
        st_ref[...] = jnp.zeros_like(st_ref)

    return _rmsnorm_rows(x_ref[0], ng_ref[...]).astype(_BF16)


def _bwd_kernel(general, x_ref, ng_ref, wq_ref, wf_ref, wi_ref, lb_ref, tri2_ref, mask_ref,
                o_ref, q_ref, v_ref, range_ref,
                p_s, qt_s, kt_s, ks_s, ed_s, eh_s, qa_s, sv_s, st_ref):
    n_chunks = x_ref.shape[1] // CHUNK
    hb = _normed_input(x_ref, ng_ref, st_ref)
    lb = _lower_bound(lb_ref[...])
    tri2 = tri2_ref[...]
    mask = mask_ref[...] > 0.5
    bufs = (qt_s, kt_s, ks_s, ed_s, eh_s, qa_s, sv_s)
    p_s[...] = jnp.dot(hb, wf_ref[...], preferred_element_type=_F32)
    qp = jnp.dot(hb, wq_ref[...], preferred_element_type=_F32)
    q_ref[0] = (qp * _sigmoid(qp)).astype(_BF16)
    v_ref[0] = jnp.dot(hb, wi_ref[...], preferred_element_type=_F32).astype(_BF16)
    for c in range(n_chunks):
        _prepare_keys(c, p_s, lb, tri2, 0, bufs)

    def emit(c, h, o):
        o_ref[0, c * CHUNK:(c + 1) * CHUNK, _head(h)] = o.astype(_BF16)

    _recurrence(general, list(reversed(range(n_chunks))), p_s, q_ref, v_ref, st_ref, lb, tri2, mask, 0,
                bufs, emit, range_ref)


def _main_kernel(general, x_ref, ob_ref, q_ref, v_ref, ng_ref, wa_ref, wf_ref, wz_ref, lng_ref, lnb_ref,
                 ws_ref, bs_ref, lb_ref, gn_ref, wo_ref, fg_ref, tri2_ref, mask_ref, y_ref, range_ref,
                 p_s, qt_s, kt_s, ks_s, ed_s, eh_s, qa_s, sv_s, z_s, mix_s, st_ref, pa_s):
    tile = x_ref.shape[1]
    n_chunks = tile // CHUNK
    hb = _normed_input(x_ref, ng_ref, st_ref)
    lb = _lower_bound(lb_ref[...])
    tri2 = tri2_ref[...]
    mask = mask_ref[...] > 0.5
    bufs = (qt_s, kt_s, ks_s, ed_s, eh_s, qa_s, sv_s)

    p_s[...] = jnp.dot(hb, wf_ref[...], preferred_element_type=_F32)

    pa_s[...] = jnp.dot(hb, wa_ref[...], preferred_element_type=_F32)
    for c in range(n_chunks):
        _prepare_keys(c, p_s, lb, tri2, CHUNK - 1, bufs)
    for n in range(tile // GMLP_CHUNK):
        rs = slice(n * GMLP_CHUNK, (n + 1) * GMLP_CHUNK)
        vv = pa_s[rs, D_MODEL:2 * D_MODEL]
        vc = vv - jnp.mean(vv, axis=-1, keepdims=True)
        vn = vc * lax.rsqrt(jnp.mean(vc * vc, axis=-1, keepdims=True) + EPS)
        vn = (vn * lng_ref[...] + lnb_ref[...]).astype(_BF16)
        for h in range(N_HEADS):
            sl = _head(h)
            s = jnp.dot(ws_ref[h], vn[:, sl], preferred_element_type=_F32) + bs_ref[:, sl]
            za = pa_s[rs, 2 * D_MODEL + h * HEAD_DIM:2 * D_MODEL + (h + 1) * HEAD_DIM]
            mix_s[rs, sl] = ((pa_s[rs, sl] * (za * _sigmoid(za))) * s).astype(_BF16)

    zb = jnp.dot(hb, wz_ref[...], preferred_element_type=_F32)
    z_s[...] = (zb * _sigmoid(zb)) * gn_ref[...]

    def emit(c, h, o):
        rows, sl = slice(c * CHUNK, (c + 1) * CHUNK), _head(h)
        o = o + ob_ref[0, rows, sl].astype(_F32)
        o = o * lax.rsqrt(jnp.mean(o * o, axis=-1, keepdims=True) + EPS)
        mix_s[rows, D_MODEL + h * HEAD_DIM:D_MODEL + (h + 1) * HEAD_DIM] = (
            o * z_s[rows, sl]).astype(_BF16)

    _recurrence(general, list(range(n_chunks)), p_s, q_ref, v_ref, st_ref, lb, tri2, mask, CHUNK - 1,
                bufs, emit, range_ref)

    out = jnp.dot(mix_s[...], wo_ref[...], preferred_element_type=_F32)
    y_ref[0] = _rmsnorm_rows(x_ref[0] + out, fg_ref[...])


def _const_spec(shape):
    return pl.BlockSpec(shape, lambda b, j: (0,) * len(shape), pipeline_mode=pl.Buffered(1))


def _w_in_spec(group, n_groups=1):
    assert group % n_groups == 0
    return pl.BlockSpec((D_MODEL, n_groups * D_MODEL), lambda b, j: (0, group // n_groups),
                        pipeline_mode=pl.Buffered(1))


def _direction_constants(reverse):
    r, c = np.indices((CHUNK, CHUNK))
    mask = ((c >= r) if reverse else (c <= r)).astype(np.float32)
    tri2 = np.concatenate([mask, mask], axis=1)
    return jnp.asarray(tri2, _BF16), jnp.asarray(mask, _F32)


def _recurrence_scratch(tile):
    n_hc = (tile // CHUNK) * N_HEADS
    act_bf16 = pltpu.VMEM((tile, D_MODEL), _BF16)
    return [
        pltpu.VMEM((tile, D_MODEL), _F32),
        act_bf16, act_bf16, act_bf16,
        pltpu.VMEM((tile, D_MODEL), _F32),
        pltpu.VMEM((tile // CHUNK, D_MODEL), _F32),
        pltpu.VMEM((n_hc, CHUNK, HEAD_DIM + CHUNK), _BF16),
        pltpu.VMEM((n_hc, HEAD_DIM + CHUNK, HEAD_DIM), _BF16),
    ]


def _trunk(x, p, tile=TILE, tile_bwd=TILE_BWD):
    y, in_range = _sweeps(x, p, tile, tile_bwd, general=False)
    return lax.cond(in_range, lambda: y, lambda: _sweeps(x, p, tile, tile_bwd, general=True)[0])


def _sweeps(x, p, tile, tile_bwd, general):
    bsz, seq, d = x.shape
    assert d == D_MODEL and seq % tile == 0 and seq % tile_bwd == 0 and tile % GMLP_CHUNK == 0
    cparams = pltpu.CompilerParams(dimension_semantics=("arbitrary", "arbitrary"),
                                   vmem_limit_bytes=VMEM_LIMIT_BYTES)
    state = pltpu.VMEM((N_HEADS, HEAD_DIM, HEAD_DIM), _F32)
    o_bwd, q_act, v_act, range_b = _bwd_sweep(general, x, p, tile_bwd, cparams, state)
    y, range_f = _main_sweep(general, x, o_bwd, q_act, v_act, p, tile, cparams, state)
    in_range = jnp.minimum(jnp.min(range_b), jnp.min(range_f)) >= _HALF_DECAY_FLOOR
    return y, in_range


def _range_out(bsz, nt):
    return (pl.BlockSpec((1, 1, 8, HEAD_DIM), lambda b, j: (b, j, 0, 0)),
            jax.ShapeDtypeStruct((bsz, nt, 8, HEAD_DIM), _F32))


def _bwd_sweep(general, x, p, tile, cparams, state):
    bsz, seq, _ = x.shape
    nt = seq // tile
    tri2_b, mask_b = _direction_constants(True)
    rev_tile = lambda b, j: (b, nt - 1 - j, 0)
    act = lambda dtype: jax.ShapeDtypeStruct((bsz, seq, D_MODEL), dtype)
    range_spec, range_shape = _range_out(bsz, nt)
    return pl.pallas_call(
        functools.partial(_bwd_kernel, general),
        grid=(bsz, nt),
        in_specs=[
            pl.BlockSpec((1, tile, D_MODEL), rev_tile),
            _const_spec((1, D_MODEL)),
            _w_in_spec(3), _w_in_spec(5), _w_in_spec(6),
            _const_spec((2, D_MODEL)),
            _const_spec((CHUNK, 2 * CHUNK)),
            _const_spec((CHUNK, CHUNK)),
        ],
        out_specs=[pl.BlockSpec((1, tile, D_MODEL), rev_tile)] * 3 + [range_spec],
        out_shape=[act(_BF16), act(_BF16), act(_BF16), range_shape],
        scratch_shapes=_recurrence_scratch(tile) + [state],
        compiler_params=cparams,
        name="hgrn_bwd_sweep_general" if general else "hgrn_bwd_sweep",
    )(x, p["norm_g"], p["w_in"], p["w_in"], p["w_in"], p["lb_bwd"], tri2_b, mask_b)


def _main_sweep(general, x, o_bwd, q_act, v_act, p, tile, cparams, state):
    bsz, seq, _ = x.shape
    nt = seq // tile
    tri2_f, mask_f = _direction_constants(False)
    fwd_tile = lambda b, j: (b, j, 0)
    range_spec, range_shape = _range_out(bsz, nt)
    return pl.pallas_call(
        functools.partial(_main_kernel, general),
        grid=(bsz, nt),
        in_specs=[
            pl.BlockSpec((1, tile, D_MODEL), fwd_tile),
            pl.BlockSpec((1, tile, D_MODEL), fwd_tile),
            pl.BlockSpec((1, tile, D_MODEL), fwd_tile),
            pl.BlockSpec((1, tile, D_MODEL), fwd_tile),
            _const_spec((1, D_MODEL)),
            _w_in_spec(0, 3), _w_in_spec(4), _w_in_spec(7),
            _const_spec((1, D_MODEL)),
            _const_spec((1, D_MODEL)),
            _const_spec((N_HEADS, GMLP_CHUNK, GMLP_CHUNK)),
            _const_spec((GMLP_CHUNK, D_MODEL)),
            _const_spec((2, D_MODEL)),
            _const_spec((1, D_MODEL)),
            _const_spec((2 * D_MODEL, D_MODEL)),
            _const_spec((1, D_MODEL)),
            _const_spec((CHUNK, 2 * CHUNK)),
            _const_spec((CHUNK, CHUNK)),
        ],
        out_specs=[pl.BlockSpec((1, tile, D_MODEL), fwd_tile), range_spec],
        out_shape=[jax.ShapeDtypeStruct((bsz, seq, D_MODEL), _F32), range_shape],
        scratch_shapes=_recurrence_scratch(tile) + [
            pltpu.VMEM((tile, D_MODEL), _F32),
            pltpu.VMEM((tile, 2 * D_MODEL), _BF16),
            state,
            pltpu.VMEM((tile, 3 * D_MODEL), _F32)],
        compiler_params=cparams,
        name="encoder_main_sweep_general" if general else "encoder_main_sweep",
    )(x, o_bwd, q_act, v_act, p["norm_g"], p["w_in"], p["w_in"], p["w_in"], p["ln_g"], p["ln_b"], p["w_s"], p["b_s"],
      p["lb_fwd"], p["gn_g"], p["w_out"], p["final_g"], tri2_f, mask_f)


def _prepare(norm_g, w_in, ln_v_g, ln_v_b, w_s, b_s, lb_params, gn_g, w_out, final_g):
    row = lambda a: a.reshape(1, -1).astype(_F32)
    return {
        "norm_g": row(norm_g[0]),
        "w_in": w_in[0].astype(_BF16),
        "ln_g": row(ln_v_g[0]),
        "ln_b": row(ln_v_b[0]),
        "w_s": w_s[0].astype(_BF16),
        "b_s": jnp.repeat(b_s[0].T.astype(_F32), HEAD_DIM, axis=1),
        "lb_fwd": lb_params[0, :, :].astype(_F32),
        "lb_bwd": lb_params[1, :, :].astype(_F32),
        "gn_g": row(gn_g[0]),
        "w_out": w_out[0].astype(_BF16),
        "final_g": row(final_g),
    }


def kernel(x_prompt, x_sample, norm_g, w_in, ln_v_g, ln_v_b, w_s, b_s, lb_params, gn_g, w_out, final_g):
    p = _prepare(norm_g, w_in, ln_v_g, ln_v_b, w_s, b_s, lb_params, gn_g, w_out, final_g)
    return (_trunk(x_prompt, p), _trunk(x_sample, p))
```

```python
import functools

import jax
import jax.numpy as jnp
import numpy as np
from jax import lax
from jax.experimental import pallas as pl
from jax.experimental.pallas import tpu as pltpu

D_MODEL = 1024
N_HEADS = 8
HEAD_DIM = 128
GMLP_CHUNK = 128
CHUNK = 64
EPS = 1e-6
TILE = 256
TILE_BWD = 256
VMEM_LIMIT_BYTES = 56 * 1024 * 1024

_BF16 = jnp.bfloat16
_F32 = jnp.float32
_LOG2E = 1.4426950408889634
_HALF_DECAY_FLOOR = 1e-26
_NT = (((1,), (1,)), ((), ()))
_TN = (((0,), (0,)), ((), ()))


def _head(h):
    return slice(h * HEAD_DIM, (h + 1) * HEAD_DIM)


def _exp(x, sign=1.0):
    return jnp.exp2(x * (sign * _LOG2E))


def _sigmoid(x):
    return 1.0 / (1.0 + _exp(x, -1.0))


def _rmsnorm_rows(x, gain):
    ms = jnp.mean(x * x, axis=-1, keepdims=True)
    return (x * lax.rsqrt(ms + EPS)) * gain


def _lower_bound(lb_pair):
    m = jnp.max(lb_pair, axis=0, keepdims=True)
    e = jnp.exp(lb_pair - m)
    return e[0:1, :] / jnp.sum(e, axis=0, keepdims=True)


def _cumsum_rows(g, tri2):
    hi = g.astype(_BF16)
    lo = (g - hi.astype(_F32)).astype(_BF16)
    g2 = jnp.concatenate([hi, lo], axis=0)
    return jnp.dot(tri2, g2, preferred_element_type=_F32)


def _split_chunk(c, f_ref, q_ref, lb, tri2, end_row, bufs):
    qt_s, kt_s, ks_s, _, eh_s, qa_s, _ = bufs
    rows = slice(c * CHUNK, (c + 1) * CHUNK)
    fg = lb + (1.0 - lb) * _sigmoid(f_ref[rows, :])
    g = jnp.log(fg)
    k = 1.0 - fg
    b = _cumsum_rows(g, tri2)
    half = 0.5 * b[end_row:end_row + 1, :]
    d = b - half
    e_half = _exp(half)
    kt = k * _exp(d, -1.0)
    kt_s[rows, :] = kt.astype(_BF16)
    ks_s[rows, :] = (kt * e_half).astype(_BF16)
    eh_s[c:c + 1, :] = e_half
    qt = q_ref[0, rows, :].astype(_F32) * _exp(d)
    qt_s[rows, :] = qt.astype(_BF16)
    qi = (qt * e_half).astype(_BF16)
    for h in range(N_HEADS):
        qa_s[c * N_HEADS + h, :, 0:HEAD_DIM] = qi[:, _head(h)]


def _general_chunk(c, f_ref, q_ref, lb, tri2, mask, end_row, bufs):
    _, _, ks_s, ed_s, _, qa_s, _ = bufs
    r0 = c * CHUNK
    rows = slice(r0, r0 + CHUNK)
    fg = lb + (1.0 - lb) * _sigmoid(f_ref[rows, :])
    g = jnp.log(fg)
    k = 1.0 - fg
    b = _cumsum_rows(g, tri2)
    total = b[end_row:end_row + 1, :]
    q = q_ref[0, rows, :].astype(_F32)
    ks_s[rows, :] = (k * _exp(total - b)).astype(_BF16)
    qi = (q * _exp(b)).astype(_BF16)
    for h in range(N_HEADS):
        qa_s[c * N_HEADS + h, :, 0:HEAD_DIM] = qi[:, _head(h)]
    ed_s[rows, :] = b
    f_ref[rows, :] = k
    col = lax.broadcasted_iota(jnp.int32, (CHUNK, CHUNK), 1)

    def one_key_row(s, acc):
        b_s = ed_s[pl.ds(r0 + s, 1), :]
        k_s = f_ref[pl.ds(r0 + s, 1), :]
        w = q * _exp(jnp.minimum(b - b_s, 0.0)) * k_s
        return tuple(a + jnp.where(col == s, jnp.sum(w[:, _head(h)], axis=-1, keepdims=True), 0.0)
                     for h, a in enumerate(acc))

    zeros = tuple(jnp.zeros((CHUNK, CHUNK), _F32) for _ in range(N_HEADS))
    acc = lax.fori_loop(0, CHUNK, one_key_row, zeros)
    for h in range(N_HEADS):
        qa_s[c * N_HEADS + h, :, HEAD_DIM:HEAD_DIM + CHUNK] = jnp.where(mask, acc[h], 0.0).astype(_BF16)


def _recurrence(general, chunk_order, f_ref, q_ref, v_ref, st_ref, lb, tri2, mask, end_row, bufs, emit,
                range_ref):
    eh_s = bufs[4]
    low = jnp.min(eh_s[...], axis=0, keepdims=True)
    low = functools.reduce(jnp.minimum, [low[:, _head(h)] for h in range(N_HEADS)])
    range_ref[0, 0] = jnp.broadcast_to(low, range_ref.shape[2:])
    if general:
        for c in sorted(chunk_order):
            _general_chunk(c, f_ref, q_ref, lb, tri2, mask, end_row, bufs)
    for c in chunk_order:
        _state_step(c, bufs, v_ref, st_ref, split_scores_mask=None if general else mask)
    for c in sorted(chunk_order):
        for h in range(N_HEADS):
            emit(c, h, _chunk_output(c, h, bufs))


def _state_step(c, bufs, v_ref, st_ref, split_scores_mask=None):
    qt_s, kt_s, ks_s, _, eh_s, qa_s, sv_s = bufs
    rows = slice(c * CHUNK, (c + 1) * CHUNK)
    e_half = eh_s[c:c + 1, :]
    decay = e_half * e_half
    for h in range(N_HEADS):
        sl = _head(h)
        i = c * N_HEADS + h
        if split_scores_mask is not None:
            scores = lax.dot_general(qt_s[rows, sl], kt_s[rows, sl], _NT, preferred_element_type=_F32)
            qa_s[i, :, HEAD_DIM:HEAD_DIM + CHUNK] = jnp.where(split_scores_mask, scores, 0.0).astype(_BF16)
        v = v_ref[0, rows, sl]
        upd = lax.dot_general(v, ks_s[rows, sl], _TN, preferred_element_type=_F32)
        st = st_ref[h]
        sv_s[i, 0:HEAD_DIM, :] = st.T.astype(_BF16)
        sv_s[i, HEAD_DIM:HEAD_DIM + CHUNK, :] = v
        st_ref[h] = st * decay[:, sl] + upd


def _chunk_output(c, h, bufs):
    qa_s, sv_s = bufs[-2:]
    i = c * N_HEADS + h
    return jnp.dot(qa_s[i], sv_s[i], preferred_element_type=_F32)


def _normed_input(x_ref, ng_ref, st_ref):
    @pl.when(pl.program_id(1) == 0)
    def _():
        st_ref[...] = jnp.zeros_like(st_ref)

    return _rmsnorm_rows(x_ref[0], ng_ref[...]).astype(_BF16)


def _bwd_kernel(general, x_ref, ng_ref, wq_ref, wf_ref, wi_ref, lb_ref, tri2_ref, mask_ref,
                o_ref, q_ref, v_ref, range_ref,
                p_s, qt_s, kt_s, ks_s, ed_s, eh_s, qa_s, sv_s, st_ref):
    n_chunks = x_ref.shape[1] // CHUNK
    hb = _normed_input(x_ref, ng_ref, st_ref)
    lb = _lower_bound(lb_ref[...])
    tri2 = tri2_ref[...]
    mask = mask_ref[...] > 0.5
    bufs = (qt_s, kt_s, ks_s, ed_s, eh_s, qa_s, sv_s)
    p_s[...] = jnp.dot(hb, wf_ref[...], preferred_element_type=_F32)
    qp = jnp.dot(hb, wq_ref[...], preferred_element_type=_F32)
    q_ref[0] = (qp * _sigmoid(qp)).astype(_BF16)
    v_ref[0] = jnp.dot(hb, wi_ref[...], preferred_element_type=_F32).astype(_BF16)
    for c in range(n_chunks):
        _split_chunk(c, p_s, q_ref, lb, tri2, 0, bufs)

    def emit(c, h, o):
        o_ref[0, c * CHUNK:(c + 1) * CHUNK, _head(h)] = o.astype(_BF16)

    _recurrence(general, list(reversed(range(n_chunks))), p_s, q_ref, v_ref, st_ref, lb, tri2, mask, 0,
                bufs, emit, range_ref)


def _main_kernel(general, x_ref, ob_ref, q_ref, v_ref, ng_ref, wa_ref, wf_ref, wz_ref, lng_ref, lnb_ref,
                 ws_ref, bs_ref, lb_ref, gn_ref, wo_ref, fg_ref, tri2_ref, mask_ref, y_ref, range_ref,
                 p_s, qt_s, kt_s, ks_s, ed_s, eh_s, qa_s, sv_s, z_s, mix_s, st_ref, pa_s):
    tile = x_ref.shape[1]
    n_chunks = tile // CHUNK
    hb = _normed_input(x_ref, ng_ref, st_ref)
    lb = _lower_bound(lb_ref[...])
    tri2 = tri2_ref[...]
    mask = mask_ref[...] > 0.5
    bufs = (qt_s, kt_s, ks_s, ed_s, eh_s, qa_s, sv_s)

    p_s[...] = jnp.dot(hb, wf_ref[...], preferred_element_type=_F32)

    pa_s[...] = jnp.dot(hb, wa_ref[...], preferred_element_type=_F32)
    for c in range(n_chunks):
        _split_chunk(c, p_s, q_ref, lb, tri2, CHUNK - 1, bufs)
    for n in range(tile // GMLP_CHUNK):
        rs = slice(n * GMLP_CHUNK, (n + 1) * GMLP_CHUNK)
        vv = pa_s[rs, D_MODEL:2 * D_MODEL]
        vc = vv - jnp.mean(vv, axis=-1, keepdims=True)
        vn = vc * lax.rsqrt(jnp.mean(vc * vc, axis=-1, keepdims=True) + EPS)
        vn = (vn * lng_ref[...] + lnb_ref[...]).astype(_BF16)
        for h in range(N_HEADS):
            sl = _head(h)
            s = jnp.dot(ws_ref[h], vn[:, sl], preferred_element_type=_F32) + bs_ref[:, sl]
            za = pa_s[rs, 2 * D_MODEL + h * HEAD_DIM:2 * D_MODEL + (h + 1) * HEAD_DIM]
            mix_s[rs, sl] = ((pa_s[rs, sl] * (za * _sigmoid(za))) * s).astype(_BF16)

    zb = jnp.dot(hb, wz_ref[...], preferred_element_type=_F32)
    z_s[...] = (zb * _sigmoid(zb)) * gn_ref[...]

    def emit(c, h, o):
        rows, sl = slice(c * CHUNK, (c + 1) * CHUNK), _head(h)
        o = o + ob_ref[0, rows, sl].astype(_F32)
        o = o * lax.rsqrt(jnp.mean(o * o, axis=-1, keepdims=True) + EPS)
        mix_s[rows, D_MODEL + h * HEAD_DIM:D_MODEL + (h + 1) * HEAD_DIM] = (
            o * z_s[rows, sl]).astype(_BF16)

    _recurrence(general, list(range(n_chunks)), p_s, q_ref, v_ref, st_ref, lb, tri2, mask, CHUNK - 1,
                bufs, emit, range_ref)

    out = jnp.dot(mix_s[...], wo_ref[...], preferred_element_type=_F32)
    y_ref[0] = _rmsnorm_rows(x_ref[0] + out, fg_ref[...])


def _const_spec(shape):
    return pl.BlockSpec(shape, lambda b, j: (0,) * len(shape), pipeline_mode=pl.Buffered(1))


def _w_in_spec(group, n_groups=1):
    assert group % n_groups == 0
    return pl.BlockSpec((D_MODEL, n_groups * D_MODEL), lambda b, j: (0, group // n_groups),
                        pipeline_mode=pl.Buffered(1))


def _direction_constants(reverse):
    r, c = np.indices((CHUNK, CHUNK))
    mask = ((c >= r) if reverse else (c <= r)).astype(np.float32)
    tri2 = np.concatenate([mask, mask], axis=1)
    return jnp.asarray(tri2, _BF16), jnp.asarray(mask, _F32)


def _recurrence_scratch(tile):
    n_hc = (tile // CHUNK) * N_HEADS
    act_bf16 = pltpu.VMEM((tile, D_MODEL), _BF16)
    return [
        pltpu.VMEM((tile, D_MODEL), _F32),
        act_bf16, act_bf16, act_bf16,
        pltpu.VMEM((tile, D_MODEL), _F32),
        pltpu.VMEM((tile // CHUNK, D_MODEL), _F32),
        pltpu.VMEM((n_hc, CHUNK, HEAD_DIM + CHUNK), _BF16),
        pltpu.VMEM((n_hc, HEAD_DIM + CHUNK, HEAD_DIM), _BF16),
    ]


def _trunk(x, p, tile=TILE, tile_bwd=TILE_BWD):
    y, in_range = _sweeps(x, p, tile, tile_bwd, general=False)
    return lax.cond(in_range, lambda y: y, lambda y: _sweeps(x, p, tile, tile_bwd, general=True)[0], y)


def _sweeps(x, p, tile, tile_bwd, general):
    bsz, seq, d = x.shape
    assert d == D_MODEL and seq % tile == 0 and seq % tile_bwd == 0 and tile % GMLP_CHUNK == 0
    cparams = pltpu.CompilerParams(dimension_semantics=("arbitrary", "arbitrary"),
                                   vmem_limit_bytes=VMEM_LIMIT_BYTES)
    state = pltpu.VMEM((N_HEADS, HEAD_DIM, HEAD_DIM), _F32)
    o_bwd, q_act, v_act, range_b = _bwd_sweep(general, x, p, tile_bwd, cparams, state)
    y, range_f = _main_sweep(general, x, o_bwd, q_act, v_act, p, tile, cparams, state)
    in_range = jnp.minimum(jnp.min(range_b), jnp.min(range_f)) >= _HALF_DECAY_FLOOR
    return y, in_range


def _range_out(bsz, nt):
    return (pl.BlockSpec((1, 1, 8, HEAD_DIM), lambda b, j: (b, j, 0, 0)),
            jax.ShapeDtypeStruct((bsz, nt, 8, HEAD_DIM), _F32))


def _bwd_sweep(general, x, p, tile, cparams, state):
    bsz, seq, _ = x.shape
    nt = seq // tile
    tri2_b, mask_b = _direction_constants(True)
    rev_tile = lambda b, j: (b, nt - 1 - j, 0)
    act = lambda dtype: jax.ShapeDtypeStruct((bsz, seq, D_MODEL), dtype)
    range_spec, range_shape = _range_out(bsz, nt)
    return pl.pallas_call(
        functools.partial(_bwd_kernel, general),
        grid=(bsz, nt),
        in_specs=[
            pl.BlockSpec((1, tile, D_MODEL), rev_tile),
            _const_spec((1, D_MODEL)),
            _w_in_spec(3), _w_in_spec(5), _w_in_spec(6),
            _const_spec((2, D_MODEL)),
            _const_spec((CHUNK, 2 * CHUNK)),
            _const_spec((CHUNK, CHUNK)),
        ],
        out_specs=[pl.BlockSpec((1, tile, D_MODEL), rev_tile)] * 3 + [range_spec],
        out_shape=[act(_BF16), act(_BF16), act(_BF16), range_shape],
        scratch_shapes=_recurrence_scratch(tile) + [state],
        compiler_params=cparams,
        name="hgrn_bwd_sweep_general" if general else "hgrn_bwd_sweep",
    )(x, p["norm_g"], p["w_in"], p["w_in"], p["w_in"], p["lb_bwd"], tri2_b, mask_b)


def _main_sweep(general, x, o_bwd, q_act, v_act, p, tile, cparams, state):
    bsz, seq, _ = x.shape
    nt = seq // tile
    tri2_f, mask_f = _direction_constants(False)
    fwd_tile = lambda b, j: (b, j, 0)
    range_spec, range_shape = _range_out(bsz, nt)
    return pl.pallas_call(
        functools.partial(_main_kernel, general),
        grid=(bsz, nt),
        in_specs=[
            pl.BlockSpec((1, tile, D_MODEL), fwd_tile),
            pl.BlockSpec((1, tile, D_MODEL), fwd_tile),
            pl.BlockSpec((1, tile, D_MODEL), fwd_tile),
            pl.BlockSpec((1, tile, D_MODEL), fwd_tile),
            _const_spec((1, D_MODEL)),
            _w_in_spec(0, 3), _w_in_spec(4), _w_in_spec(7),
            _const_spec((1, D_MODEL)),
            _const_spec((1, D_MODEL)),
            _const_spec((N_HEADS, GMLP_CHUNK, GMLP_CHUNK)),
            _const_spec((GMLP_CHUNK, D_MODEL)),
            _const_spec((2, D_MODEL)),
            _const_spec((1, D_MODEL)),
            _const_spec((2 * D_MODEL, D_MODEL)),
            _const_spec((1, D_MODEL)),
            _const_spec((CHUNK, 2 * CHUNK)),
            _const_spec((CHUNK, CHUNK)),
        ],
        out_specs=[pl.BlockSpec((1, tile, D_MODEL), fwd_tile), range_spec],
        out_shape=[jax.ShapeDtypeStruct((bsz, seq, D_MODEL), _F32), range_shape],
        scratch_shapes=_recurrence_scratch(tile) + [
            pltpu.VMEM((tile, D_MODEL), _F32),
            pltpu.VMEM((tile, 2 * D_MODEL), _BF16),
            state,
            pltpu.VMEM((tile, 3 * D_MODEL), _F32)],
        compiler_params=cparams,
        name="encoder_main_sweep_general" if general else "encoder_main_sweep",
    )(x, o_bwd, q_act, v_act, p["norm_g"], p["w_in"], p["w_in"], p["w_in"], p["ln_g"], p["ln_b"], p["w_s"], p["b_s"],
      p["lb_fwd"], p["gn_g"], p["w_out"], p["final_g"], tri2_f, mask_f)


def _prepare(norm_g, w_in, ln_v_g, ln_v_b, w_s, b_s, lb_params, gn_g, w_out, final_g):
    row = lambda a: a.reshape(1, -1).astype(_F32)
    return {
        "norm_g": row(norm_g[0]),
        "w_in": w_in[0].astype(_BF16),
        "ln_g": row(ln_v_g[0]),
        "ln_b": row(ln_v_b[0]),
        "w_s": w_s[0].astype(_BF16),
        "b_s": jnp.repeat(b_s[0].T.astype(_F32), HEAD_DIM, axis=1),
        "lb_fwd": lb_params[0, :, :].astype(_F32),
        "lb_bwd": lb_params[1, :, :].astype(_F32),
        "gn_g": row(gn_g[0]),
        "w_out": w_out[0].astype(_BF16),
        "final_g": row(final_g),
    }


def kernel(x_prompt, x_sample, norm_g, w_in, ln_v_g, ln_v_b, w_s, b_s, lb_params, gn_g, w_out, final_g):
    p = _prepare(norm_g, w_in, ln_v_g, ln_v_b, w_s, b_s, lb_params, gn_g, w_out, final_g)
    return (_trunk(x_prompt, p), _trunk(x_sample, p))
```

```python
import functools

import jax
import jax.numpy as jnp
import numpy as np
from jax import lax
from jax.experimental import pallas as pl
from jax.experimental.pallas import tpu as pltpu

D_MODEL = 1024
N_HEADS = 8
HEAD_DIM = 128
GMLP_CHUNK = 128
CHUNK = 128
EPS = 1e-6
TILE = 256
TILE_BWD = 256
VMEM_LIMIT_BYTES = 56 * 1024 * 1024

_BF16 = jnp.bfloat16
_F32 = jnp.float32
_LOG2E = 1.4426950408889634
_HALF_DECAY_FLOOR = 1e-26
_NT = (((1,), (1,)), ((), ()))
_TN = (((0,), (0,)), ((), ()))


def _head(h):
    return slice(h * HEAD_DIM, (h + 1) * HEAD_DIM)


def _exp(x, sign=1.0):
    return jnp.exp2(x * (sign * _LOG2E))


def _sigmoid(x):
    return 1.0 / (1.0 + _exp(x, -1.0))


def _rmsnorm_rows(x, gain):
    ms = jnp.mean(x * x, axis=-1, keepdims=True)
    return (x * lax.rsqrt(ms + EPS)) * gain


def _lower_bound(lb_pair):
    m = jnp.max(lb_pair, axis=0, keepdims=True)
    e = jnp.exp(lb_pair - m)
    return e[0:1, :] / jnp.sum(e, axis=0, keepdims=True)


def _cumsum_rows(g, tri2):
    hi = g.astype(_BF16)
    lo = (g - hi.astype(_F32)).astype(_BF16)
    g2 = jnp.concatenate([hi, lo], axis=0)
    return jnp.dot(tri2, g2, preferred_element_type=_F32)


def _split_chunk(c, f_ref, q_ref, lb, tri2, end_row, bufs):
    qt_s, kt_s, ks_s, _, eh_s, qa_s, _ = bufs
    rows = slice(c * CHUNK, (c + 1) * CHUNK)
    fg = lb + (1.0 - lb) * _sigmoid(f_ref[rows, :])
    g = jnp.log(fg)
    k = 1.0 - fg
    b = _cumsum_rows(g, tri2)
    half = 0.5 * b[end_row:end_row + 1, :]
    d = b - half
    e_half = _exp(half)
    kt = k * _exp(d, -1.0)
    kt_s[rows, :] = kt.astype(_BF16)
    ks_s[rows, :] = (kt * e_half).astype(_BF16)
    eh_s[c:c + 1, :] = e_half
    qt = q_ref[0, rows, :].astype(_F32) * _exp(d)
    qt_s[rows, :] = qt.astype(_BF16)
    qi = (qt * e_half).astype(_BF16)
    for h in range(N_HEADS):
        qa_s[c * N_HEADS + h, :, 0:HEAD_DIM] = qi[:, _head(h)]


def _general_chunk(c, f_ref, q_ref, lb, tri2, mask, end_row, bufs):
    _, _, ks_s, ed_s, _, qa_s, _ = bufs
    r0 = c * CHUNK
    rows = slice(r0, r0 + CHUNK)
    fg = lb + (1.0 - lb) * _sigmoid(f_ref[rows, :])
    g = jnp.log(fg)
    k = 1.0 - fg
    b = _cumsum_rows(g, tri2)
    total = b[end_row:end_row + 1, :]
    q = q_ref[0, rows, :].astype(_F32)
    ks_s[rows, :] = (k * _exp(total - b)).astype(_BF16)
    qi = (q * _exp(b)).astype(_BF16)
    for h in range(N_HEADS):
        qa_s[c * N_HEADS + h, :, 0:HEAD_DIM] = qi[:, _head(h)]
    ed_s[rows, :] = b
    f_ref[rows, :] = k
    col = lax.broadcasted_iota(jnp.int32, (CHUNK, CHUNK), 1)

    def one_key_row(s, acc):
        b_s = ed_s[pl.ds(r0 + s, 1), :]
        k_s = f_ref[pl.ds(r0 + s, 1), :]
        w = q * _exp(jnp.minimum(b - b_s, 0.0)) * k_s
        return tuple(a + jnp.where(col == s, jnp.sum(w[:, _head(h)], axis=-1, keepdims=True), 0.0)
                     for h, a in enumerate(acc))

    zeros = tuple(jnp.zeros((CHUNK, CHUNK), _F32) for _ in range(N_HEADS))
    acc = lax.fori_loop(0, CHUNK, one_key_row, zeros)
    for h in range(N_HEADS):
        qa_s[c * N_HEADS + h, :, HEAD_DIM:HEAD_DIM + CHUNK] = jnp.where(mask, acc[h], 0.0).astype(_BF16)


def _recurrence(general, chunk_order, f_ref, q_ref, v_ref, st_ref, lb, tri2, mask, end_row, bufs, emit,
                range_ref):
    eh_s = bufs[4]
    low = jnp.min(eh_s[...], axis=0, keepdims=True)
    low = functools.reduce(jnp.minimum, [low[:, _head(h)] for h in range(N_HEADS)])
    range_ref[0, 0] = jnp.broadcast_to(low, range_ref.shape[2:])
    if general:
        for c in sorted(chunk_order):
            _general_chunk(c, f_ref, q_ref, lb, tri2, mask, end_row, bufs)
    for c in chunk_order:
        _state_step(c, bufs, v_ref, st_ref, split_scores_mask=None if general else mask)
    for c in sorted(chunk_order):
        for h in range(N_HEADS):
            emit(c, h, _chunk_output(c, h, bufs))


def _state_step(c, bufs, v_ref, st_ref, split_scores_mask=None):
    qt_s, kt_s, ks_s, _, eh_s, qa_s, sv_s = bufs
    rows = slice(c * CHUNK, (c + 1) * CHUNK)
    e_half = eh_s[c:c + 1, :]
    decay = e_half * e_half
    for h in range(N_HEADS):
        sl = _head(h)
        i = c * N_HEADS + h
        if split_scores_mask is not None:
            scores = lax.dot_general(qt_s[rows, sl], kt_s[rows, sl], _NT, preferred_element_type=_F32)
            qa_s[i, :, HEAD_DIM:HEAD_DIM + CHUNK] = jnp.where(split_scores_mask, scores, 0.0).astype(_BF16)
        v = v_ref[0, rows, sl]
        upd = lax.dot_general(v, ks_s[rows, sl], _TN, preferred_element_type=_F32)
        st = st_ref[h]
        sv_s[i, 0:HEAD_DIM, :] = st.T.astype(_BF16)
        sv_s[i, HEAD_DIM:HEAD_DIM + CHUNK, :] = v
        st_ref[h] = st * decay[:, sl] + upd


def _chunk_output(c, h, bufs):
    qa_s, sv_s = bufs[-2:]
    i = c * N_HEADS + h
    return jnp.dot(qa_s[i], sv_s[i], preferred_element_type=_F32)


def _normed_input(x_ref, ng_ref, st_ref):
    @pl.when(pl.program_id(1) == 0)
    def _():
        st_ref[...] = jnp.zeros_like(st_ref)

    return _rmsnorm_rows(x_ref[0], ng_ref[...]).astype(_BF16)


def _bwd_kernel(general, x_ref, ng_ref, wq_ref, wf_ref, wi_ref, lb_ref, tri2_ref, mask_ref,
                o_ref, q_ref, v_ref, range_ref,
                p_s, qt_s, kt_s, ks_s, ed_s, eh_s, qa_s, sv_s, st_ref):
    n_chunks = x_ref.shape[1] // CHUNK
    hb = _normed_input(x_ref, ng_ref, st_ref)
    lb = _lower_bound(lb_ref[...])
    tri2 = tri2_ref[...]
    mask = mask_ref[...] > 0.5
    bufs = (qt_s, kt_s, ks_s, ed_s, eh_s, qa_s, sv_s)
    p_s[...] = jnp.dot(hb, wf_ref[...], preferred_element_type=_F32)
    qp = jnp.dot(hb, wq_ref[...], preferred_element_type=_F32)
    q_ref[0] = (qp * _sigmoid(qp)).astype(_BF16)
    v_ref[0] = jnp.dot(hb, wi_ref[...], preferred_element_type=_F32).astype(_BF16)
    for c in range(n_chunks):
        _split_chunk(c, p_s, q_ref, lb, tri2, 0, bufs)

    def emit(c, h, o):
        o_ref[0, c * CHUNK:(c + 1) * CHUNK, _head(h)] = o.astype(_BF16)

    _recurrence(general, list(reversed(range(n_chunks))), p_s, q_ref, v_ref, st_ref, lb, tri2, mask, 0,
                bufs, emit, range_ref)


def _main_kernel(general, x_ref, ob_ref, q_ref, v_ref, ng_ref, wa_ref, wf_ref, wz_ref, lng_ref, lnb_ref,
                 ws_ref, bs_ref, lb_ref, gn_ref, wo_ref, fg_ref, tri2_ref, mask_ref, y_ref, range_ref,
                 p_s, qt_s, kt_s, ks_s, ed_s, eh_s, qa_s, sv_s, z_s, mix_s, st_ref, pa_s):
    tile = x_ref.shape[1]
    n_chunks = tile // CHUNK
    hb = _normed_input(x_ref, ng_ref, st_ref)
    lb = _lower_bound(lb_ref[...])
    tri2 = tri2_ref[...]
    mask = mask_ref[...] > 0.5
    bufs = (qt_s, kt_s, ks_s, ed_s, eh_s, qa_s, sv_s)

    p_s[...] = jnp.dot(hb, wf_ref[...], preferred_element_type=_F32)

    pa_s[...] = jnp.dot(hb, wa_ref[...], preferred_element_type=_F32)
    for c in range(n_chunks):
        _split_chunk(c, p_s, q_ref, lb, tri2, CHUNK - 1, bufs)
    for n in range(tile // GMLP_CHUNK):
        rs = slice(n * GMLP_CHUNK, (n + 1) * GMLP_CHUNK)
        vv = pa_s[rs, D_MODEL:2 * D_MODEL]
        vc = vv - jnp.mean(vv, axis=-1, keepdims=True)
        vn = vc * lax.rsqrt(jnp.mean(vc * vc, axis=-1, keepdims=True) + EPS)
        vn = (vn * lng_ref[...] + lnb_ref[...]).astype(_BF16)
        for h in range(N_HEADS):
            sl = _head(h)
            s = jnp.dot(ws_ref[h], vn[:, sl], preferred_element_type=_F32) + bs_ref[:, sl]
            za = pa_s[rs, 2 * D_MODEL + h * HEAD_DIM:2 * D_MODEL + (h + 1) * HEAD_DIM]
            mix_s[rs, sl] = ((pa_s[rs, sl] * (za * _sigmoid(za))) * s).astype(_BF16)

    zb = jnp.dot(hb, wz_ref[...], preferred_element_type=_F32)
    z_s[...] = (zb * _sigmoid(zb)) * gn_ref[...]

    def emit(c, h, o):
        rows, sl = slice(c * CHUNK, (c + 1) * CHUNK), _head(h)
        o = o + ob_ref[0, rows, sl].astype(_F32)
        o = o * lax.rsqrt(jnp.mean(o * o, axis=-1, keepdims=True) + EPS)
        mix_s[rows, D_MODEL + h * HEAD_DIM:D_MODEL + (h + 1) * HEAD_DIM] = (
            o * z_s[rows, sl]).astype(_BF16)

    _recurrence(general, list(range(n_chunks)), p_s, q_ref, v_ref, st_ref, lb, tri2, mask, CHUNK - 1,
                bufs, emit, range_ref)

    out = jnp.dot(mix_s[...], wo_ref[...], preferred_element_type=_F32)
    y_ref[0] = _rmsnorm_rows(x_ref[0] + out, fg_ref[...])


def _const_spec(shape):
    return pl.BlockSpec(shape, lambda b, j: (0,) * len(shape), pipeline_mode=pl.Buffered(1))


def _w_in_spec(group, n_groups=1):
    assert group % n_groups == 0
    return pl.BlockSpec((D_MODEL, n_groups * D_MODEL), lambda b, j: (0, group // n_groups),
                        pipeline_mode=pl.Buffered(1))


def _direction_constants(reverse):
    r, c = np.indices((CHUNK, CHUNK))
    mask = ((c >= r) if reverse else (c <= r)).astype(np.float32)
    tri2 = np.concatenate([mask, mask], axis=1)
    return jnp.asarray(tri2, _BF16), jnp.asarray(mask, _F32)


def _recurrence_scratch(tile):
    n_hc = (tile // CHUNK) * N_HEADS
    act_bf16 = pltpu.VMEM((tile, D_MODEL), _BF16)
    return [
        pltpu.VMEM((tile, D_MODEL), _F32),
        act_bf16, act_bf16, act_bf16,
        pltpu.VMEM((tile, D_MODEL), _F32),
        pltpu.VMEM((tile // CHUNK, D_MODEL), _F32),
        pltpu.VMEM((n_hc, CHUNK, HEAD_DIM + CHUNK), _BF16),
        pltpu.VMEM((n_hc, HEAD_DIM + CHUNK, HEAD_DIM), _BF16),
    ]


def _trunk(x, p, tile=TILE, tile_bwd=TILE_BWD):
    y, in_range = _sweeps(x, p, tile, tile_bwd, general=False)
    return lax.cond(in_range, lambda y: y, lambda y: _sweeps(x, p, tile, tile_bwd, general=True)[0], y)


def _sweeps(x, p, tile, tile_bwd, general):
    bsz, seq, d = x.shape
    assert d == D_MODEL and seq % tile == 0 and seq % tile_bwd == 0 and tile % GMLP_CHUNK == 0
    cparams = pltpu.CompilerParams(dimension_semantics=("arbitrary", "arbitrary"),
                                   vmem_limit_bytes=VMEM_LIMIT_BYTES)
    state = pltpu.VMEM((N_HEADS, HEAD_DIM, HEAD_DIM), _F32)
    o_bwd, q_act, v_act, range_b = _bwd_sweep(general, x, p, tile_bwd, cparams, state)
    y, range_f = _main_sweep(general, x, o_bwd, q_act, v_act, p, tile, cparams, state)
    in_range = jnp.minimum(jnp.min(range_b), jnp.min(range_f)) >= _HALF_DECAY_FLOOR
    return y, in_range


def _range_out(bsz, nt):
    return (pl.BlockSpec((1, 1, 8, HEAD_DIM), lambda b, j: (b, j, 0, 0)),
            jax.ShapeDtypeStruct((bsz, nt, 8, HEAD_DIM), _F32))


def _bwd_sweep(general, x, p, tile, cparams, state):
    bsz, seq, _ = x.shape
    nt = seq // tile
    tri2_b, mask_b = _direction_constants(True)
    rev_tile = lambda b, j: (b, nt - 1 - j, 0)
    act = lambda dtype: jax.ShapeDtypeStruct((bsz, seq, D_MODEL), dtype)
    range_spec, range_shape = _range_out(bsz, nt)
    return pl.pallas_call(
        functools.partial(_bwd_kernel, general),
        grid=(bsz, nt),
        in_specs=[
            pl.BlockSpec((1, tile, D_MODEL), rev_tile),
            _const_spec((1, D_MODEL)),
            _w_in_spec(3), _w_in_spec(5), _w_in_spec(6),
            _const_spec((2, D_MODEL)),
            _const_spec((CHUNK, 2 * CHUNK)),
            _const_spec((CHUNK, CHUNK)),
        ],
        out_specs=[pl.BlockSpec((1, tile, D_MODEL), rev_tile)] * 3 + [range_spec],
        out_shape=[act(_BF16), act(_BF16), act(_BF16), range_shape],
        scratch_shapes=_recurrence_scratch(tile) + [state],
        compiler_params=cparams,
        name="hgrn_bwd_sweep_general" if general else "hgrn_bwd_sweep",
    )(x, p["norm_g"], p["w_in"], p["w_in"], p["w_in"], p["lb_bwd"], tri2_b, mask_b)


def _main_sweep(general, x, o_bwd, q_act, v_act, p, tile, cparams, state):
    bsz, seq, _ = x.shape
    nt = seq // tile
    tri2_f, mask_f = _direction_constants(False)
    fwd_tile = lambda b, j: (b, j, 0)
    range_spec, range_shape = _range_out(bsz, nt)
    return pl.pallas_call(
        functools.partial(_main_kernel, general),
        grid=(bsz, nt),
        in_specs=[
            pl.BlockSpec((1, tile, D_MODEL), fwd_tile),
            pl.BlockSpec((1, tile, D_MODEL), fwd_tile),
            pl.BlockSpec((1, tile, D_MODEL), fwd_tile),
            pl.BlockSpec((1, tile, D_MODEL), fwd_tile),
            _const_spec((1, D_MODEL)),
            _w_in_spec(0, 3), _w_in_spec(4), _w_in_spec(7),
            _const_spec((1, D_MODEL)),
            _const_spec((1, D_MODEL)),
            _const_spec((N_HEADS, GMLP_CHUNK, GMLP_CHUNK)),
            _const_spec((GMLP_CHUNK, D_MODEL)),
            _const_spec((2, D_MODEL)),
            _const_spec((1, D_MODEL)),
            _const_spec((2 * D_MODEL, D_MODEL)),
            _const_spec((1, D_MODEL)),
            _const_spec((CHUNK, 2 * CHUNK)),
            _const_spec((CHUNK, CHUNK)),
        ],
        out_specs=[pl.BlockSpec((1, tile, D_MODEL), fwd_tile), range_spec],
        out_shape=[jax.ShapeDtypeStruct((bsz, seq, D_MODEL), _F32), range_shape],
        scratch_shapes=_recurrence_scratch(tile) + [
            pltpu.VMEM((tile, D_MODEL), _F32),
            pltpu.VMEM((tile, 2 * D_MODEL), _BF16),
            state,
            pltpu.VMEM((tile, 3 * D_MODEL), _F32)],
        compiler_params=cparams,
        name="encoder_main_sweep_general" if general else "encoder_main_sweep",
    )(x, o_bwd, q_act, v_act, p["norm_g"], p["w_in"], p["w_in"], p["w_in"], p["ln_g"], p["ln_b"], p["w_s"], p["b_s"],
      p["lb_fwd"], p["gn_g"], p["w_out"], p["final_g"], tri2_f, mask_f)


def _prepare(norm_g, w_in, ln_v_g, ln_v_b, w_s, b_s, lb_params, gn_g, w_out, final_g):
    row = lambda a: a.reshape(1, -1).astype(_F32)
    return {
        "norm_g": row(norm_g[0]),
        "w_in": w_in[0].astype(_BF16),
        "ln_g": row(ln_v_g[0]),
        "ln_b": row(ln_v_b[0]),
        "w_s": w_s[0].astype(_BF16),
        "b_s": jnp.repeat(b_s[0].T.astype(_F32), HEAD_DIM, axis=1),
        "lb_fwd": lb_params[0, :, :].astype(_F32),
        "lb_bwd": lb_params[1, :, :].astype(_F32),
        "gn_g": row(gn_g[0]),
        "w_out": w_out[0].astype(_BF16),
        "final_g": row(final_g),
    }


def kernel(x_prompt, x_sample, norm_g, w_in, ln_v_g, ln_v_b, w_s, b_s, lb_params, gn_g, w_out, final_g):
    p = _prepare(norm_g, w_in, ln_v_g, ln_v_b, w_s, b_s, lb_params, gn_g, w_out, final_g)
    return (_trunk(x_prompt, p), _trunk(x_sample, p))
```

```python
import functools

import jax
import jax.numpy as jnp
import numpy as np
from jax import lax
from jax.experimental import pallas as pl
from jax.experimental.pallas import tpu as pltpu

D_MODEL = 1024
N_HEADS = 8
HEAD_DIM = 128
GMLP_CHUNK = 128
CHUNK = 128
EPS = 1e-6
TILE = 256
TILE_BWD = 512
VMEM_LIMIT_BYTES = 56 * 1024 * 1024

_BF16 = jnp.bfloat16
_F32 = jnp.float32
_LOG2E = 1.4426950408889634
_HALF_DECAY_FLOOR = 1e-26
_NT = (((1,), (1,)), ((), ()))
_TN = (((0,), (0,)), ((), ()))


def _head(h):
    return slice(h * HEAD_DIM, (h + 1) * HEAD_DIM)


def _exp(x, sign=1.0):
    return jnp.exp2(x * (sign * _LOG2E))


def _sigmoid(x):
    return 1.0 / (1.0 + _exp(x, -1.0))


def _rmsnorm_rows(x, gain):
    ms = jnp.mean(x * x, axis=-1, keepdims=True)
    return (x * lax.rsqrt(ms + EPS)) * gain


def _lower_bound(lb_pair):
    m = jnp.max(lb_pair, axis=0, keepdims=True)
    e = jnp.exp(lb_pair - m)
    return e[0:1, :] / jnp.sum(e, axis=0, keepdims=True)


def _cumsum_rows(g, tri2):
    hi = g.astype(_BF16)
    lo = (g - hi.astype(_F32)).astype(_BF16)
    g2 = jnp.concatenate([hi, lo], axis=0)
    return jnp.dot(tri2, g2, preferred_element_type=_F32)


def _split_chunk(c, f_ref, q_ref, lb, tri2, end_row, bufs):
    qt_s, kt_s, ks_s, _, eh_s, qa_s, _ = bufs
    rows = slice(c * CHUNK, (c + 1) * CHUNK)
    fg = lb + (1.0 - lb) * _sigmoid(f_ref[rows, :])
    g = jnp.log(fg)
    k = 1.0 - fg
    b = _cumsum_rows(g, tri2)
    half = 0.5 * b[end_row:end_row + 1, :]
    d = b - half
    e_half = _exp(half)
    kt = k * _exp(d, -1.0)
    kt_s[rows, :] = kt.astype(_BF16)
    ks_s[rows, :] = (kt * e_half).astype(_BF16)
    eh_s[c:c + 1, :] = e_half
    qt = q_ref[0, rows, :].astype(_F32) * _exp(d)
    qt_s[rows, :] = qt.astype(_BF16)
    qi = (qt * e_half).astype(_BF16)
    for h in range(N_HEADS):
        qa_s[c * N_HEADS + h, :, 0:HEAD_DIM] = qi[:, _head(h)]


def _general_chunk(c, f_ref, q_ref, lb, tri2, mask, end_row, bufs):
    _, _, ks_s, ed_s, _, qa_s, _ = bufs
    r0 = c * CHUNK
    rows = slice(r0, r0 + CHUNK)
    fg = lb + (1.0 - lb) * _sigmoid(f_ref[rows, :])
    g = jnp.log(fg)
    k = 1.0 - fg
    b = _cumsum_rows(g, tri2)
    total = b[end_row:end_row + 1, :]
    q = q_ref[0, rows, :].astype(_F32)
    ks_s[rows, :] = (k * _exp(total - b)).astype(_BF16)
    qi = (q * _exp(b)).astype(_BF16)
    for h in range(N_HEADS):
        qa_s[c * N_HEADS + h, :, 0:HEAD_DIM] = qi[:, _head(h)]
    ed_s[rows, :] = b
    f_ref[rows, :] = k
    col = lax.broadcasted_iota(jnp.int32, (CHUNK, CHUNK), 1)

    def one_key_row(s, acc):
        b_s = ed_s[pl.ds(r0 + s, 1), :]
        k_s = f_ref[pl.ds(r0 + s, 1), :]
        w = q * _exp(jnp.minimum(b - b_s, 0.0)) * k_s
        return tuple(a + jnp.where(col == s, jnp.sum(w[:, _head(h)], axis=-1, keepdims=True), 0.0)
                     for h, a in enumerate(acc))

    zeros = tuple(jnp.zeros((CHUNK, CHUNK), _F32) for _ in range(N_HEADS))
    acc = lax.fori_loop(0, CHUNK, one_key_row, zeros)
    for h in range(N_HEADS):
        qa_s[c * N_HEADS + h, :, HEAD_DIM:HEAD_DIM + CHUNK] = jnp.where(mask, acc[h], 0.0).astype(_BF16)


def _recurrence(general, chunk_order, f_ref, q_ref, v_ref, st_ref, lb, tri2, mask, end_row, bufs, emit,
                range_ref):
    eh_s = bufs[4]
    low = jnp.min(eh_s[...], axis=0, keepdims=True)
    low = functools.reduce(jnp.minimum, [low[:, _head(h)] for h in range(N_HEADS)])
    range_ref[0, 0] = jnp.broadcast_to(low, range_ref.shape[2:])
    if general:
        for c in sorted(chunk_order):
            _general_chunk(c, f_ref, q_ref, lb, tri2, mask, end_row, bufs)
    for c in chunk_order:
        _state_step(c, bufs, v_ref, st_ref, split_scores_mask=None if general else mask)
    for c in sorted(chunk_order):
        for h in range(N_HEADS):
            emit(c, h, _chunk_output(c, h, bufs))


def _state_step(c, bufs, v_ref, st_ref, split_scores_mask=None):
    qt_s, kt_s, ks_s, _, eh_s, qa_s, sv_s = bufs
    rows = slice(c * CHUNK, (c + 1) * CHUNK)
    e_half = eh_s[c:c + 1, :]
    decay = e_half * e_half
    for h in range(N_HEADS):
        sl = _head(h)
        i = c * N_HEADS + h
        if split_scores_mask is not None:
            scores = lax.dot_general(qt_s[rows, sl], kt_s[rows, sl], _NT, preferred_element_type=_F32)
            qa_s[i, :, HEAD_DIM:HEAD_DIM + CHUNK] = jnp.where(split_scores_mask, scores, 0.0).astype(_BF16)
        v = v_ref[0, rows, sl]
        upd = lax.dot_general(v, ks_s[rows, sl], _TN, preferred_element_type=_F32)
        st = st_ref[h]
        sv_s[i, 0:HEAD_DIM, :] = st.T.astype(_BF16)
        sv_s[i, HEAD_DIM:HEAD_DIM + CHUNK, :] = v
        st_ref[h] = st * decay[:, sl] + upd


def _chunk_output(c, h, bufs):
    qa_s, sv_s = bufs[-2:]
    i = c * N_HEADS + h
    return jnp.dot(qa_s[i], sv_s[i], preferred_element_type=_F32)


def _normed_input(x_ref, ng_ref, st_ref):
    @pl.when(pl.program_id(1) == 0)
    def _():
        st_ref[...] = jnp.zeros_like(st_ref)

    return _rmsnorm_rows(x_ref[0], ng_ref[...]).astype(_BF16)


def _bwd_kernel(general, x_ref, ng_ref, wq_ref, wf_ref, wi_ref, lb_ref, tri2_ref, mask_ref,
                o_ref, q_ref, v_ref, range_ref,
                p_s, qt_s, kt_s, ks_s, ed_s, eh_s, qa_s, sv_s, st_ref):
    n_chunks = x_ref.shape[1] // CHUNK
    hb = _normed_input(x_ref, ng_ref, st_ref)
    lb = _lower_bound(lb_ref[...])
    tri2 = tri2_ref[...]
    mask = mask_ref[...] > 0.5
    bufs = (qt_s, kt_s, ks_s, ed_s, eh_s, qa_s, sv_s)
    p_s[...] = jnp.dot(hb, wf_ref[...], preferred_element_type=_F32)
    qp = jnp.dot(hb, wq_ref[...], preferred_element_type=_F32)
    q_ref[0] = (qp * _sigmoid(qp)).astype(_BF16)
    v_ref[0] = jnp.dot(hb, wi_ref[...], preferred_element_type=_F32).astype(_BF16)
    for c in range(n_chunks):
        _split_chunk(c, p_s, q_ref, lb, tri2, 0, bufs)

    def emit(c, h, o):
        o_ref[0, c * CHUNK:(c + 1) * CHUNK, _head(h)] = o.astype(_BF16)

    _recurrence(general, list(reversed(range(n_chunks))), p_s, q_ref, v_ref, st_ref, lb, tri2, mask, 0,
                bufs, emit, range_ref)


def _main_kernel(general, x_ref, ob_ref, q_ref, v_ref, ng_ref, wa_ref, wf_ref, wz_ref, lng_ref, lnb_ref,
                 ws_ref, bs_ref, lb_ref, gn_ref, wo_ref, fg_ref, tri2_ref, mask_ref, y_ref, range_ref,
                 p_s, qt_s, kt_s, ks_s, ed_s, eh_s, qa_s, sv_s, z_s, mix_s, st_ref, pa_s):
    tile = x_ref.shape[1]
    n_chunks = tile // CHUNK
    hb = _normed_input(x_ref, ng_ref, st_ref)
    lb = _lower_bound(lb_ref[...])
    tri2 = tri2_ref[...]
    mask = mask_ref[...] > 0.5
    bufs = (qt_s, kt_s, ks_s, ed_s, eh_s, qa_s, sv_s)

    p_s[...] = jnp.dot(hb, wf_ref[...], preferred_element_type=_F32)

    pa_s[...] = jnp.dot(hb, wa_ref[...], preferred_element_type=_F32)
    for c in range(n_chunks):
        _split_chunk(c, p_s, q_ref, lb, tri2, CHUNK - 1, bufs)
    for n in range(tile // GMLP_CHUNK):
        rs = slice(n * GMLP_CHUNK, (n + 1) * GMLP_CHUNK)
        vv = pa_s[rs, D_MODEL:2 * D_MODEL]
        vc = vv - jnp.mean(vv, axis=-1, keepdims=True)
        vn = vc * lax.rsqrt(jnp.mean(vc * vc, axis=-1, keepdims=True) + EPS)
        vn = (vn * lng_ref[...] + lnb_ref[...]).astype(_BF16)
        for h in range(N_HEADS):
            sl = _head(h)
            s = jnp.dot(ws_ref[h], vn[:, sl], preferred_element_type=_F32) + bs_ref[:, sl]
            za = pa_s[rs, 2 * D_MODEL + h * HEAD_DIM:2 * D_MODEL + (h + 1) * HEAD_DIM]
            mix_s[rs, sl] = ((pa_s[rs, sl] * (za * _sigmoid(za))) * s).astype(_BF16)

    zb = jnp.dot(hb, wz_ref[...], preferred_element_type=_F32)
    z_s[...] = (zb * _sigmoid(zb)) * gn_ref[...]

    def emit(c, h, o):
        rows, sl = slice(c * CHUNK, (c + 1) * CHUNK), _head(h)
        o = o + ob_ref[0, rows, sl].astype(_F32)
        o = o * lax.rsqrt(jnp.mean(o * o, axis=-1, keepdims=True) + EPS)
        mix_s[rows, D_MODEL + h * HEAD_DIM:D_MODEL + (h + 1) * HEAD_DIM] = (
            o * z_s[rows, sl]).astype(_BF16)

    _recurrence(general, list(range(n_chunks)), p_s, q_ref, v_ref, st_ref, lb, tri2, mask, CHUNK - 1,
                bufs, emit, range_ref)

    out = jnp.dot(mix_s[...], wo_ref[...], preferred_element_type=_F32)
    y_ref[0] = _rmsnorm_rows(x_ref[0] + out, fg_ref[...])


def _const_spec(shape):
    return pl.BlockSpec(shape, lambda b, j: (0,) * len(shape), pipeline_mode=pl.Buffered(1))


def _w_in_spec(group, n_groups=1):
    assert group % n_groups == 0
    return pl.BlockSpec((D_MODEL, n_groups * D_MODEL), lambda b, j: (0, group // n_groups),
                        pipeline_mode=pl.Buffered(1))


def _direction_constants(reverse):
    r, c = np.indices((CHUNK, CHUNK))
    mask = ((c >= r) if reverse else (c <= r)).astype(np.float32)
    tri2 = np.concatenate([mask, mask], axis=1)
    return jnp.asarray(tri2, _BF16), jnp.asarray(mask, _F32)


def _recurrence_scratch(tile):
    n_hc = (tile // CHUNK) * N_HEADS
    act_bf16 = pltpu.VMEM((tile, D_MODEL), _BF16)
    return [
        pltpu.VMEM((tile, D_MODEL), _F32),
        act_bf16, act_bf16, act_bf16,
        pltpu.VMEM((tile, D_MODEL), _F32),
        pltpu.VMEM((tile // CHUNK, D_MODEL), _F32),
        pltpu.VMEM((n_hc, CHUNK, HEAD_DIM + CHUNK), _BF16),
        pltpu.VMEM((n_hc, HEAD_DIM + CHUNK, HEAD_DIM), _BF16),
    ]


def _trunk(x, p, tile=TILE, tile_bwd=TILE_BWD):
    y, in_range = _sweeps(x, p, tile, tile_bwd, general=False)
    return lax.cond(in_range, lambda y: y, lambda y: _sweeps(x, p, tile, tile_bwd, general=True)[0], y)


def _sweeps(x, p, tile, tile_bwd, general):
    bsz, seq, d = x.shape
    assert d == D_MODEL and seq % tile == 0 and seq % tile_bwd == 0 and tile % GMLP_CHUNK == 0
    cparams = pltpu.CompilerParams(dimension_semantics=("arbitrary", "arbitrary"),
                                   vmem_limit_bytes=VMEM_LIMIT_BYTES)
    state = pltpu.VMEM((N_HEADS, HEAD_DIM, HEAD_DIM), _F32)
    o_bwd, q_act, v_act, range_b = _bwd_sweep(general, x, p, tile_bwd, cparams, state)
    y, range_f = _main_sweep(general, x, o_bwd, q_act, v_act, p, tile, cparams, state)
    in_range = jnp.minimum(jnp.min(range_b), jnp.min(range_f)) >= _HALF_DECAY_FLOOR
    return y, in_range


def _range_out(bsz, nt):
    return (pl.BlockSpec((1, 1, 8, HEAD_DIM), lambda b, j: (b, j, 0, 0)),
            jax.ShapeDtypeStruct((bsz, nt, 8, HEAD_DIM), _F32))


def _bwd_sweep(general, x, p, tile, cparams, state):
    bsz, seq, _ = x.shape
    nt = seq // tile
    tri2_b, mask_b = _direction_constants(True)
    rev_tile = lambda b, j: (b, nt - 1 - j, 0)
    act = lambda dtype: jax.ShapeDtypeStruct((bsz, seq, D_MODEL), dtype)
    range_spec, range_shape = _range_out(bsz, nt)
    return pl.pallas_call(
        functools.partial(_bwd_kernel, general),
        grid=(bsz, nt),
        in_specs=[
            pl.BlockSpec((1, tile, D_MODEL), rev_tile),
            _const_spec((1, D_MODEL)),
            _w_in_spec(3), _w_in_spec(5), _w_in_spec(6),
            _const_spec((2, D_MODEL)),
            _const_spec((CHUNK, 2 * CHUNK)),
            _const_spec((CHUNK, CHUNK)),
        ],
        out_specs=[pl.BlockSpec((1, tile, D_MODEL), rev_tile)] * 3 + [range_spec],
        out_shape=[act(_BF16), act(_BF16), act(_BF16), range_shape],
        scratch_shapes=_recurrence_scratch(tile) + [state],
        compiler_params=cparams,
        name="hgrn_bwd_sweep_general" if general else "hgrn_bwd_sweep",
    )(x, p["norm_g"], p["w_in"], p["w_in"], p["w_in"], p["lb_bwd"], tri2_b, mask_b)


def _main_sweep(general, x, o_bwd, q_act, v_act, p, tile, cparams, state):
    bsz, seq, _ = x.shape
    nt = seq // tile
    tri2_f, mask_f = _direction_constants(False)
    fwd_tile = lambda b, j: (b, j, 0)
    range_spec, range_shape = _range_out(bsz, nt)
    return pl.pallas_call(
        functools.partial(_main_kernel, general),
        grid=(bsz, nt),
        in_specs=[
            pl.BlockSpec((1, tile, D_MODEL), fwd_tile),
            pl.BlockSpec((1, tile, D_MODEL), fwd_tile),
            pl.BlockSpec((1, tile, D_MODEL), fwd_tile),
            pl.BlockSpec((1, tile, D_MODEL), fwd_tile),
            _const_spec((1, D_MODEL)),
            _w_in_spec(0, 3), _w_in_spec(4), _w_in_spec(7),
            _const_spec((1, D_MODEL)),
            _const_spec((1, D_MODEL)),
            _const_spec((N_HEADS, GMLP_CHUNK, GMLP_CHUNK)),
            _const_spec((GMLP_CHUNK, D_MODEL)),
            _const_spec((2, D_MODEL)),
            _const_spec((1, D_MODEL)),
            _const_spec((2 * D_MODEL, D_MODEL)),
            _const_spec((1, D_MODEL)),
            _const_spec((CHUNK, 2 * CHUNK)),
            _const_spec((CHUNK, CHUNK)),
        ],
        out_specs=[pl.BlockSpec((1, tile, D_MODEL), fwd_tile), range_spec],
        out_shape=[jax.ShapeDtypeStruct((bsz, seq, D_MODEL), _F32), range_shape],
        scratch_shapes=_recurrence_scratch(tile) + [
            pltpu.VMEM((tile, D_MODEL), _F32),
            pltpu.VMEM((tile, 2 * D_MODEL), _BF16),
            state,
            pltpu.VMEM((tile, 3 * D_MODEL), _F32)],
        compiler_params=cparams,
        name="encoder_main_sweep_general" if general else "encoder_main_sweep",
    )(x, o_bwd, q_act, v_act, p["norm_g"], p["w_in"], p["w_in"], p["w_in"], p["ln_g"], p["ln_b"], p["w_s"], p["b_s"],
      p["lb_fwd"], p["gn_g"], p["w_out"], p["final_g"], tri2_f, mask_f)


def _prepare(norm_g, w_in, ln_v_g, ln_v_b, w_s, b_s, lb_params, gn_g, w_out, final_g):
    row = lambda a: a.reshape(1, -1).astype(_F32)
    return {
        "norm_g": row(norm_g[0]),
        "w_in": w_in[0].astype(_BF16),
        "ln_g": row(ln_v_g[0]),
        "ln_b": row(ln_v_b[0]),
        "w_s": w_s[0].astype(_BF16),
        "b_s": jnp.repeat(b_s[0].T.astype(_F32), HEAD_DIM, axis=1),
        "lb_fwd": lb_params[0, :, :].astype(_F32),
        "lb_bwd": lb_params[1, :, :].astype(_F32),
        "gn_g": row(gn_g[0]),
        "w_out": w_out[0].astype(_BF16),
        "final_g": row(final_g),
    }


def kernel(x_prompt, x_sample, norm_g, w_in, ln_v_g, ln_v_b, w_s, b_s, lb_params, gn_g, w_out, final_g):
    p = _prepare(norm_g, w_in, ln_v_g, ln_v_b, w_s, b_s, lb_params, gn_g, w_out, final_g)
    return (_trunk(x_prompt, p), _trunk(x_sample, p))
```

```python
import functools

import jax
import jax.numpy as jnp
import numpy as np
from jax import lax
from jax.experimental import pallas as pl
from jax.experimental.pallas import tpu as pltpu

D_MODEL = 1024
N_HEADS = 8
HEAD_DIM = 128
GMLP_CHUNK = 128
CHUNK = 128
CHUNK_NARROW = 64
EPS = 1e-6
TILE = 256
TILE_BWD = 512
VMEM_LIMIT_BYTES = 56 * 1024 * 1024

_BF16 = jnp.bfloat16
_F32 = jnp.float32
_LOG2E = 1.4426950408889634
_HALF_DECAY_FLOOR = 1e-26
_NT = (((1,), (1,)), ((), ()))
_TN = (((0,), (0,)), ((), ()))


def _head(h):
    return slice(h * HEAD_DIM, (h + 1) * HEAD_DIM)


def _exp(x, sign=1.0):
    return jnp.exp2(x * (sign * _LOG2E))


def _sigmoid(x):
    return 1.0 / (1.0 + _exp(x, -1.0))


def _rmsnorm_rows(x, gain):
    ms = jnp.mean(x * x, axis=-1, keepdims=True)
    return (x * lax.rsqrt(ms + EPS)) * gain


def _lower_bound(lb_pair):
    m = jnp.max(lb_pair, axis=0, keepdims=True)
    e = jnp.exp(lb_pair - m)
    return e[0:1, :] / jnp.sum(e, axis=0, keepdims=True)


def _cumsum_rows(g, tri2):
    hi = g.astype(_BF16)
    lo = (g - hi.astype(_F32)).astype(_BF16)
    g2 = jnp.concatenate([hi, lo], axis=0)
    return jnp.dot(tri2, g2, preferred_element_type=_F32)


def _split_chunk(c, f_ref, q_ref, lb, tri2, end_row, bufs):
    qt_s, kt_s, ks_s, _, eh_s, qa_s, _ = bufs
    chunk = qa_s.shape[1]
    rows = slice(c * chunk, (c + 1) * chunk)
    fg = lb + (1.0 - lb) * _sigmoid(f_ref[rows, :])
    g = jnp.log(fg)
    k = 1.0 - fg
    b = _cumsum_rows(g, tri2)
    half = 0.5 * b[end_row:end_row + 1, :]
    d = b - half
    e_half = _exp(half)
    kt = k * _exp(d, -1.0)
    kt_s[rows, :] = kt.astype(_BF16)
    ks_s[rows, :] = (kt * e_half).astype(_BF16)
    eh_s[c:c + 1, :] = e_half
    qt = q_ref[0, rows, :].astype(_F32) * _exp(d)
    qt_s[rows, :] = qt.astype(_BF16)
    qi = (qt * e_half).astype(_BF16)
    for h in range(N_HEADS):
        qa_s[c * N_HEADS + h, :, 0:HEAD_DIM] = qi[:, _head(h)]


def _general_chunk(c, f_ref, q_ref, lb, tri2, mask, end_row, bufs):
    _, _, ks_s, ed_s, _, qa_s, _ = bufs
    chunk = qa_s.shape[1]
    r0 = c * chunk
    rows = slice(r0, r0 + chunk)
    fg = lb + (1.0 - lb) * _sigmoid(f_ref[rows, :])
    g = jnp.log(fg)
    k = 1.0 - fg
    b = _cumsum_rows(g, tri2)
    total = b[end_row:end_row + 1, :]
    q = q_ref[0, rows, :].astype(_F32)
    ks_s[rows, :] = (k * _exp(total - b)).astype(_BF16)
    qi = (q * _exp(b)).astype(_BF16)
    for h in range(N_HEADS):
        qa_s[c * N_HEADS + h, :, 0:HEAD_DIM] = qi[:, _head(h)]
    ed_s[rows, :] = b
    f_ref[rows, :] = k
    col = lax.broadcasted_iota(jnp.int32, (chunk, chunk), 1)

    def one_key_row(s, acc):
        b_s = ed_s[pl.ds(r0 + s, 1), :]
        k_s = f_ref[pl.ds(r0 + s, 1), :]
        w = q * _exp(jnp.minimum(b - b_s, 0.0)) * k_s
        return tuple(a + jnp.where(col == s, jnp.sum(w[:, _head(h)], axis=-1, keepdims=True), 0.0)
                     for h, a in enumerate(acc))

    zeros = tuple(jnp.zeros((chunk, chunk), _F32) for _ in range(N_HEADS))
    acc = lax.fori_loop(0, chunk, one_key_row, zeros)
    for h in range(N_HEADS):
        qa_s[c * N_HEADS + h, :, HEAD_DIM:HEAD_DIM + chunk] = jnp.where(mask, acc[h], 0.0).astype(_BF16)


def _recurrence(general, chunk_order, f_ref, q_ref, v_ref, st_ref, lb, tri2, mask, end_row, bufs, emit,
                range_ref):
    eh_s = bufs[4]
    low = jnp.min(eh_s[...], axis=0, keepdims=True)
    low = functools.reduce(jnp.minimum, [low[:, _head(h)] for h in range(N_HEADS)])
    range_ref[0, 0] = jnp.broadcast_to(low, range_ref.shape[2:])
    if general:
        for c in sorted(chunk_order):
            _general_chunk(c, f_ref, q_ref, lb, tri2, mask, end_row, bufs)
    for c in chunk_order:
        _state_step(c, bufs, v_ref, st_ref, split_scores_mask=None if general else mask)
    for c in sorted(chunk_order):
        for h in range(N_HEADS):
            emit(c, h, _chunk_output(c, h, bufs))


def _state_step(c, bufs, v_ref, st_ref, split_scores_mask=None):
    qt_s, kt_s, ks_s, _, eh_s, qa_s, sv_s = bufs
    chunk = qa_s.shape[1]
    rows = slice(c * chunk, (c + 1) * chunk)
    e_half = eh_s[c:c + 1, :]
    decay = e_half * e_half
    for h in range(N_HEADS):
        sl = _head(h)
        i = c * N_HEADS + h
        if split_scores_mask is not None:
            scores = lax.dot_general(qt_s[rows, sl], kt_s[rows, sl], _NT, preferred_element_type=_F32)
            qa_s[i, :, HEAD_DIM:HEAD_DIM + chunk] = jnp.where(split_scores_mask, scores, 0.0).astype(_BF16)
        v = v_ref[0, rows, sl]
        upd = lax.dot_general(v, ks_s[rows, sl], _TN, preferred_element_type=_F32)
        st = st_ref[h]
        sv_s[i, 0:HEAD_DIM, :] = st.T.astype(_BF16)
        sv_s[i, HEAD_DIM:HEAD_DIM + chunk, :] = v
        st_ref[h] = st * decay[:, sl] + upd


def _chunk_output(c, h, bufs):
    qa_s, sv_s = bufs[-2:]
    i = c * N_HEADS + h
    return jnp.dot(qa_s[i], sv_s[i], preferred_element_type=_F32)


def _normed_input(x_ref, ng_ref, st_ref):
    @pl.when(pl.program_id(1) == 0)
    def _():
        st_ref[...] = jnp.zeros_like(st_ref)

    return _rmsnorm_rows(x_ref[0], ng_ref[...]).astype(_BF16)


def _bwd_kernel(general, x_ref, ng_ref, wq_ref, wf_ref, wi_ref, lb_ref, tri2_ref, mask_ref,
                o_ref, q_ref, v_ref, range_ref,
                p_s, qt_s, kt_s, ks_s, ed_s, eh_s, qa_s, sv_s, st_ref):
    chunk = qa_s.shape[1]
    n_chunks = x_ref.shape[1] // chunk
    hb = _normed_input(x_ref, ng_ref, st_ref)
    lb = _lower_bound(lb_ref[...])
    tri2 = tri2_ref[...]
    mask = mask_ref[...] > 0.5
    bufs = (qt_s, kt_s, ks_s, ed_s, eh_s, qa_s, sv_s)
    p_s[...] = jnp.dot(hb, wf_ref[...], preferred_element_type=_F32)
    qp = jnp.dot(hb, wq_ref[...], preferred_element_type=_F32)
    q_ref[0] = (qp * _sigmoid(qp)).astype(_BF16)
    v_ref[0] = jnp.dot(hb, wi_ref[...], preferred_element_type=_F32).astype(_BF16)
    for c in range(n_chunks):
        _split_chunk(c, p_s, q_ref, lb, tri2, 0, bufs)

    def emit(c, h, o):
        o_ref[0, c * chunk:(c + 1) * chunk, _head(h)] = o.astype(_BF16)

    _recurrence(general, list(reversed(range(n_chunks))), p_s, q_ref, v_ref, st_ref, lb, tri2, mask, 0,
                bufs, emit, range_ref)


def _main_kernel(general, x_ref, ob_ref, q_ref, v_ref, ng_ref, wa_ref, wf_ref, wz_ref, lng_ref, lnb_ref,
                 ws_ref, bs_ref, lb_ref, gn_ref, wo_ref, fg_ref, tri2_ref, mask_ref, y_ref, range_ref,
                 p_s, qt_s, kt_s, ks_s, ed_s, eh_s, qa_s, sv_s, z_s, mix_s, st_ref, pa_s):
    tile = x_ref.shape[1]
    chunk = qa_s.shape[1]
    n_chunks = tile // chunk
    hb = _normed_input(x_ref, ng_ref, st_ref)
    lb = _lower_bound(lb_ref[...])
    tri2 = tri2_ref[...]
    mask = mask_ref[...] > 0.5
    bufs = (qt_s, kt_s, ks_s, ed_s, eh_s, qa_s, sv_s)

    p_s[...] = jnp.dot(hb, wf_ref[...], preferred_element_type=_F32)

    pa_s[...] = jnp.dot(hb, wa_ref[...], preferred_element_type=_F32)
    for c in range(n_chunks):
        _split_chunk(c, p_s, q_ref, lb, tri2, chunk - 1, bufs)
    for n in range(tile // GMLP_CHUNK):
        rs = slice(n * GMLP_CHUNK, (n + 1) * GMLP_CHUNK)
        vv = pa_s[rs, D_MODEL:2 * D_MODEL]
        vc = vv - jnp.mean(vv, axis=-1, keepdims=True)
        vn = vc * lax.rsqrt(jnp.mean(vc * vc, axis=-1, keepdims=True) + EPS)
        vn = (vn * lng_ref[...] + lnb_ref[...]).astype(_BF16)
        for h in range(N_HEADS):
            sl = _head(h)
            s = jnp.dot(ws_ref[h], vn[:, sl], preferred_element_type=_F32) + bs_ref[:, sl]
            za = pa_s[rs, 2 * D_MODEL + h * HEAD_DIM:2 * D_MODEL + (h + 1) * HEAD_DIM]
            mix_s[rs, sl] = ((pa_s[rs, sl] * (za * _sigmoid(za))) * s).astype(_BF16)

    zb = jnp.dot(hb, wz_ref[...], preferred_element_type=_F32)
    z_s[...] = (zb * _sigmoid(zb)) * gn_ref[...]

    def emit(c, h, o):
        rows, sl = slice(c * chunk, (c + 1) * chunk), _head(h)
        o = o + ob_ref[0, rows, sl].astype(_F32)
        o = o * lax.rsqrt(jnp.mean(o * o, axis=-1, keepdims=True) + EPS)
        mix_s[rows, D_MODEL + h * HEAD_DIM:D_MODEL + (h + 1) * HEAD_DIM] = (
            o * z_s[rows, sl]).astype(_BF16)

    _recurrence(general, list(range(n_chunks)), p_s, q_ref, v_ref, st_ref, lb, tri2, mask, chunk - 1,
                bufs, emit, range_ref)

    out = jnp.dot(mix_s[...], wo_ref[...], preferred_element_type=_F32)
    y_ref[0] = _rmsnorm_rows(x_ref[0] + out, fg_ref[...])


def _const_spec(shape):
    return pl.BlockSpec(shape, lambda b, j: (0,) * len(shape), pipeline_mode=pl.Buffered(1))


def _w_in_spec(group, n_groups=1):
    assert group % n_groups == 0
    return pl.BlockSpec((D_MODEL, n_groups * D_MODEL), lambda b, j: (0, group // n_groups),
                        pipeline_mode=pl.Buffered(1))


def _direction_constants(reverse, chunk):
    r, c = np.indices((chunk, chunk))
    mask = ((c >= r) if reverse else (c <= r)).astype(np.float32)
    tri2 = np.concatenate([mask, mask], axis=1)
    return jnp.asarray(tri2, _BF16), jnp.asarray(mask, _F32)


def _recurrence_scratch(tile, chunk):
    n_hc = (tile // chunk) * N_HEADS
    act_bf16 = pltpu.VMEM((tile, D_MODEL), _BF16)
    return [
        pltpu.VMEM((tile, D_MODEL), _F32),
        act_bf16, act_bf16, act_bf16,
        pltpu.VMEM((tile, D_MODEL), _F32),
        pltpu.VMEM((tile // chunk, D_MODEL), _F32),
        pltpu.VMEM((n_hc, chunk, HEAD_DIM + chunk), _BF16),
        pltpu.VMEM((n_hc, HEAD_DIM + chunk, HEAD_DIM), _BF16),
    ]


def _trunk(x, p, tile=TILE, tile_bwd=TILE_BWD):
    def narrow(_):
        y, in_range = _sweeps(x, p, tile, tile_bwd, CHUNK_NARROW, general=False)
        safe = lambda _: _sweeps(x, p, tile, tile_bwd, CHUNK_NARROW, general=True)[0]
        return lax.cond(in_range, lambda y: y, safe, y)

    y, in_range = _sweeps(x, p, tile, tile_bwd, CHUNK, general=False)
    return lax.cond(in_range, lambda y: y, narrow, y)


def _sweeps(x, p, tile, tile_bwd, chunk, general):
    bsz, seq, d = x.shape
    assert d == D_MODEL and seq % tile == 0 and seq % tile_bwd == 0 and tile % GMLP_CHUNK == 0
    cparams = pltpu.CompilerParams(dimension_semantics=("arbitrary", "arbitrary"),
                                   vmem_limit_bytes=VMEM_LIMIT_BYTES)
    state = pltpu.VMEM((N_HEADS, HEAD_DIM, HEAD_DIM), _F32)
    o_bwd, q_act, v_act, range_b = _bwd_sweep(general, x, p, tile_bwd, chunk, cparams, state)
    y, range_f = _main_sweep(general, x, o_bwd, q_act, v_act, p, tile, chunk, cparams, state)
    in_range = jnp.minimum(jnp.min(range_b), jnp.min(range_f)) >= _HALF_DECAY_FLOOR
    return y, in_range


def _range_out(bsz, nt):
    return (pl.BlockSpec((1, 1, 8, HEAD_DIM), lambda b, j: (b, j, 0, 0)),
            jax.ShapeDtypeStruct((bsz, nt, 8, HEAD_DIM), _F32))


def _bwd_sweep(general, x, p, tile, chunk, cparams, state):
    bsz, seq, _ = x.shape
    nt = seq // tile
    tri2_b, mask_b = _direction_constants(True, chunk)
    rev_tile = lambda b, j: (b, nt - 1 - j, 0)
    act = lambda dtype: jax.ShapeDtypeStruct((bsz, seq, D_MODEL), dtype)
    range_spec, range_shape = _range_out(bsz, nt)
    return pl.pallas_call(
        functools.partial(_bwd_kernel, general),
        grid=(bsz, nt),
        in_specs=[
            pl.BlockSpec((1, tile, D_MODEL), rev_tile),
            _const_spec((1, D_MODEL)),
            _w_in_spec(3), _w_in_spec(5), _w_in_spec(6),
            _const_spec((2, D_MODEL)),
            _const_spec((chunk, 2 * chunk)),
            _const_spec((chunk, chunk)),
        ],
        out_specs=[pl.BlockSpec((1, tile, D_MODEL), rev_tile)] * 3 + [range_spec],
        out_shape=[act(_BF16), act(_BF16), act(_BF16), range_shape],
        scratch_shapes=_recurrence_scratch(tile, chunk) + [state],
        compiler_params=cparams,
        name=f"hgrn_bwd_sweep_c{chunk}" + ("_general" if general else ""),
    )(x, p["norm_g"], p["w_in"], p["w_in"], p["w_in"], p["lb_bwd"], tri2_b, mask_b)


def _main_sweep(general, x, o_bwd, q_act, v_act, p, tile, chunk, cparams, state):
    bsz, seq, _ = x.shape
    nt = seq // tile
    tri2_f, mask_f = _direction_constants(False, chunk)
    fwd_tile = lambda b, j: (b, j, 0)
    range_spec, range_shape = _range_out(bsz, nt)
    return pl.pallas_call(
        functools.partial(_main_kernel, general),
        grid=(bsz, nt),
        in_specs=[
            pl.BlockSpec((1, tile, D_MODEL), fwd_tile),
            pl.BlockSpec((1, tile, D_MODEL), fwd_tile),
            pl.BlockSpec((1, tile, D_MODEL), fwd_tile),
            pl.BlockSpec((1, tile, D_MODEL), fwd_tile),
            _const_spec((1, D_MODEL)),
            _w_in_spec(0, 3), _w_in_spec(4), _w_in_spec(7),
            _const_spec((1, D_MODEL)),
            _const_spec((1, D_MODEL)),
            _const_spec((N_HEADS, GMLP_CHUNK, GMLP_CHUNK)),
            _const_spec((GMLP_CHUNK, D_MODEL)),
            _const_spec((2, D_MODEL)),
            _const_spec((1, D_MODEL)),
            _const_spec((2 * D_MODEL, D_MODEL)),
            _const_spec((1, D_MODEL)),
            _const_spec((chunk, 2 * chunk)),
            _const_spec((chunk, chunk)),
        ],
        out_specs=[pl.BlockSpec((1, tile, D_MODEL), fwd_tile), range_spec],
        out_shape=[jax.ShapeDtypeStruct((bsz, seq, D_MODEL), _F32), range_shape],
        scratch_shapes=_recurrence_scratch(tile, chunk) + [
            pltpu.VMEM((tile, D_MODEL), _F32),
            pltpu.VMEM((tile, 2 * D_MODEL), _BF16),
            state,
            pltpu.VMEM((tile, 3 * D_MODEL), _F32)],
        compiler_params=cparams,
        name=f"encoder_main_sweep_c{chunk}" + ("_general" if general else ""),
    )(x, o_bwd, q_act, v_act, p["norm_g"], p["w_in"], p["w_in"], p["w_in"], p["ln_g"], p["ln_b"], p["w_s"], p["b_s"],
      p["lb_fwd"], p["gn_g"], p["w_out"], p["final_g"], tri2_f, mask_f)


def _prepare(norm_g, w_in, ln_v_g, ln_v_b, w_s, b_s, lb_params, gn_g, w_out, final_g):
    row = lambda a: a.reshape(1, -1).astype(_F32)
    return {
        "norm_g": row(norm_g[0]),
        "w_in": w_in[0].astype(_BF16),
        "ln_g": row(ln_v_g[0]),
        "ln_b": row(ln_v_b[0]),
        "w_s": w_s[0].astype(_BF16),
        "b_s": jnp.repeat(b_s[0].T.astype(_F32), HEAD_DIM, axis=1),
        "lb_fwd": lb_params[0, :, :].astype(_F32),
        "lb_bwd": lb_params[1, :, :].astype(_F32),
        "gn_g": row(gn_g[0]),
        "w_out": w_out[0].astype(_BF16),
        "final_g": row(final_g),
    }


def kernel(x_prompt, x_sample, norm_g, w_in, ln_v_g, ln_v_b, w_s, b_s, lb_params, gn_g, w_out, final_g):
    p = _prepare(norm_g, w_in, ln_v_g, ln_v_b, w_s, b_s, lb_params, gn_g, w_out, final_g)
    return (_trunk(x_prompt, p), _trunk(x_sample, p))
```

```python
import functools

import jax
import jax.numpy as jnp
import numpy as np
from jax import lax
from jax.experimental import pallas as pl
from jax.experimental.pallas import tpu as pltpu

D_MODEL = 1024
N_HEADS = 8
HEAD_DIM = 128
GMLP_CHUNK = 128
CHUNK = 128
CHUNK_NARROW = 64
EPS = 1e-6
TILE = 256
TILE_BWD = 512
VMEM_LIMIT_BYTES = 56 * 1024 * 1024

_BF16 = jnp.bfloat16
_F32 = jnp.float32
_LOG2E = 1.4426950408889634
_HALF_DECAY_FLOOR = 1e-26
_NT = (((1,), (1,)), ((), ()))
_TN = (((0,), (0,)), ((), ()))


def _head(h):
    return slice(h * HEAD_DIM, (h + 1) * HEAD_DIM)


def _exp(x, sign=1.0):
    return jnp.exp2(x * (sign * _LOG2E))


def _sigmoid(x):
    return 1.0 / (1.0 + _exp(x, -1.0))


def _rmsnorm_rows(x, gain):
    ms = jnp.mean(x * x, axis=-1, keepdims=True)
    return (x * lax.rsqrt(ms + EPS)) * gain


def _lower_bound(lb_pair):
    m = jnp.max(lb_pair, axis=0, keepdims=True)
    e = jnp.exp(lb_pair - m)
    return e[0:1, :] / jnp.sum(e, axis=0, keepdims=True)


def _cumsum_rows(g, tri2):
    hi = g.astype(_BF16)
    lo = (g - hi.astype(_F32)).astype(_BF16)
    g2 = jnp.concatenate([hi, lo], axis=0)
    return jnp.dot(tri2, g2, preferred_element_type=_F32)


def _split_chunk(c, f_ref, q_ref, lb, tri2, end_row, bufs):
    qt_s, kt_s, ks_s, _, eh_s, qa_s, _ = bufs
    chunk = qa_s.shape[1]
    rows = slice(c * chunk, (c + 1) * chunk)
    fg = lb + (1.0 - lb) * _sigmoid(f_ref[rows, :])
    g = jnp.log(fg)
    k = 1.0 - fg
    b = _cumsum_rows(g, tri2)
    half = 0.5 * b[end_row:end_row + 1, :]
    d = b - half
    e_half = _exp(half)
    kt = k * _exp(d, -1.0)
    kt_s[rows, :] = kt.astype(_BF16)
    ks_s[rows, :] = (kt * e_half).astype(_BF16)
    eh_s[c:c + 1, :] = e_half
    qt = q_ref[0, rows, :].astype(_F32) * _exp(d)
    qt_s[rows, :] = qt.astype(_BF16)
    qi = (qt * e_half).astype(_BF16)
    for h in range(N_HEADS):
        qa_s[c * N_HEADS + h, :, 0:HEAD_DIM] = qi[:, _head(h)]


def _general_chunk(c, f_ref, q_ref, lb, tri2, mask, end_row, bufs):
    _, _, ks_s, ed_s, _, qa_s, _ = bufs
    chunk = qa_s.shape[1]
    r0 = c * chunk
    rows = slice(r0, r0 + chunk)
    fg = lb + (1.0 - lb) * _sigmoid(f_ref[rows, :])
    g = jnp.log(fg)
    k = 1.0 - fg
    b = _cumsum_rows(g, tri2)
    total = b[end_row:end_row + 1, :]
    q = q_ref[0, rows, :].astype(_F32)
    ks_s[rows, :] = (k * _exp(total - b)).astype(_BF16)
    qi = (q * _exp(b)).astype(_BF16)
    for h in range(N_HEADS):
        qa_s[c * N_HEADS + h, :, 0:HEAD_DIM] = qi[:, _head(h)]
    ed_s[rows, :] = b
    f_ref[rows, :] = k
    col = lax.broadcasted_iota(jnp.int32, (chunk, chunk), 1)

    def one_key_row(s, acc):
        b_s = ed_s[pl.ds(r0 + s, 1), :]
        k_s = f_ref[pl.ds(r0 + s, 1), :]
        w = q * _exp(jnp.minimum(b - b_s, 0.0)) * k_s
        return tuple(a + jnp.where(col == s, jnp.sum(w[:, _head(h)], axis=-1, keepdims=True), 0.0)
                     for h, a in enumerate(acc))

    zeros = tuple(jnp.zeros((chunk, chunk), _F32) for _ in range(N_HEADS))
    acc = lax.fori_loop(0, chunk, one_key_row, zeros)
    for h in range(N_HEADS):
        qa_s[c * N_HEADS + h, :, HEAD_DIM:HEAD_DIM + chunk] = jnp.where(mask, acc[h], 0.0).astype(_BF16)


def _recurrence(general, chunk_order, f_ref, q_ref, v_ref, st_ref, lb, tri2, mask, end_row, bufs, emit,
                range_ref):
    eh_s = bufs[4]
    low = jnp.min(eh_s[...], axis=0, keepdims=True)
    low = functools.reduce(jnp.minimum, [low[:, _head(h)] for h in range(N_HEADS)])
    range_ref[...] = jnp.minimum(range_ref[...], low)
    if general:
        for c in sorted(chunk_order):
            _general_chunk(c, f_ref, q_ref, lb, tri2, mask, end_row, bufs)
    for c in chunk_order:
        _state_step(c, bufs, v_ref, st_ref, split_scores_mask=None if general else mask)
    for c in sorted(chunk_order):
        for h in range(N_HEADS):
            emit(c, h, _chunk_output(c, h, bufs))


def _state_step(c, bufs, v_ref, st_ref, split_scores_mask=None):
    qt_s, kt_s, ks_s, _, eh_s, qa_s, sv_s = bufs
    chunk = qa_s.shape[1]
    rows = slice(c * chunk, (c + 1) * chunk)
    e_half = eh_s[c:c + 1, :]
    decay = e_half * e_half
    for h in range(N_HEADS):
        sl = _head(h)
        i = c * N_HEADS + h
        if split_scores_mask is not None:
            scores = lax.dot_general(qt_s[rows, sl], kt_s[rows, sl], _NT, preferred_element_type=_F32)
            qa_s[i, :, HEAD_DIM:HEAD_DIM + chunk] = jnp.where(split_scores_mask, scores, 0.0).astype(_BF16)
        v = v_ref[0, rows, sl]
        upd = lax.dot_general(v, ks_s[rows, sl], _TN, preferred_element_type=_F32)
        st = st_ref[h]
        sv_s[i, 0:HEAD_DIM, :] = st.T.astype(_BF16)
        sv_s[i, HEAD_DIM:HEAD_DIM + chunk, :] = v
        st_ref[h] = st * decay[:, sl] + upd


def _chunk_output(c, h, bufs):
    qa_s, sv_s = bufs[-2:]
    i = c * N_HEADS + h
    return jnp.dot(qa_s[i], sv_s[i], preferred_element_type=_F32)


def _normed_input(x_ref, ng_ref, st_ref, range_ref):
    @pl.when(pl.program_id(1) == 0)
    def _():
        st_ref[...] = jnp.zeros_like(st_ref)

    @pl.when((pl.program_id(0) == 0) & (pl.program_id(1) == 0))
    def _():
        range_ref[...] = jnp.full(range_ref.shape, jnp.inf, range_ref.dtype)

    return _rmsnorm_rows(x_ref[0], ng_ref[...]).astype(_BF16)


def _bwd_kernel(general, x_ref, ng_ref, wq_ref, wf_ref, wi_ref, lb_ref, tri2_ref, mask_ref,
                o_ref, q_ref, v_ref, range_ref,
                p_s, qt_s, kt_s, ks_s, ed_s, eh_s, qa_s, sv_s, st_ref):
    chunk = qa_s.shape[1]
    n_chunks = x_ref.shape[1] // chunk
    hb = _normed_input(x_ref, ng_ref, st_ref, range_ref)
    lb = _lower_bound(lb_ref[...])
    tri2 = tri2_ref[...]
    mask = mask_ref[...] > 0.5
    bufs = (qt_s, kt_s, ks_s, ed_s, eh_s, qa_s, sv_s)
    p_s[...] = jnp.dot(hb, wf_ref[...], preferred_element_type=_F32)
    qp = jnp.dot(hb, wq_ref[...], preferred_element_type=_F32)
    q_ref[0] = (qp * _sigmoid(qp)).astype(_BF16)
    v_ref[0] = jnp.dot(hb, wi_ref[...], preferred_element_type=_F32).astype(_BF16)
    for c in range(n_chunks):
        _split_chunk(c, p_s, q_ref, lb, tri2, 0, bufs)

    def emit(c, h, o):
        o_ref[0, c * chunk:(c + 1) * chunk, _head(h)] = o.astype(_BF16)

    _recurrence(general, list(reversed(range(n_chunks))), p_s, q_ref, v_ref, st_ref, lb, tri2, mask, 0,
                bufs, emit, range_ref)


def _main_kernel(general, x_ref, ob_ref, q_ref, v_ref, ng_ref, wa_ref, wf_ref, wz_ref, lng_ref, lnb_ref,
                 ws_ref, bs_ref, lb_ref, gn_ref, wo_ref, fg_ref, tri2_ref, mask_ref, y_ref, range_ref,
                 p_s, qt_s, kt_s, ks_s, ed_s, eh_s, qa_s, sv_s, z_s, mix_s, st_ref, pa_s):
    tile = x_ref.shape[1]
    chunk = qa_s.shape[1]
    n_chunks = tile // chunk
    hb = _normed_input(x_ref, ng_ref, st_ref, range_ref)
    lb = _lower_bound(lb_ref[...])
    tri2 = tri2_ref[...]
    mask = mask_ref[...] > 0.5
    bufs = (qt_s, kt_s, ks_s, ed_s, eh_s, qa_s, sv_s)

    p_s[...] = jnp.dot(hb, wf_ref[...], preferred_element_type=_F32)

    pa_s[...] = jnp.dot(hb, wa_ref[...], preferred_element_type=_F32)
    for c in range(n_chunks):
        _split_chunk(c, p_s, q_ref, lb, tri2, chunk - 1, bufs)
    for n in range(tile // GMLP_CHUNK):
        rs = slice(n * GMLP_CHUNK, (n + 1) * GMLP_CHUNK)
        vv = pa_s[rs, D_MODEL:2 * D_MODEL]
        vc = vv - jnp.mean(vv, axis=-1, keepdims=True)
        vn = vc * lax.rsqrt(jnp.mean(vc * vc, axis=-1, keepdims=True) + EPS)
        vn = (vn * lng_ref[...] + lnb_ref[...]).astype(_BF16)
        for h in range(N_HEADS):
            sl = _head(h)
            s = jnp.dot(ws_ref[h], vn[:, sl], preferred_element_type=_F32) + bs_ref[:, sl]
            za = pa_s[rs, 2 * D_MODEL + h * HEAD_DIM:2 * D_MODEL + (h + 1) * HEAD_DIM]
            mix_s[rs, sl] = ((pa_s[rs, sl] * (za * _sigmoid(za))) * s).astype(_BF16)

    zb = jnp.dot(hb, wz_ref[...], preferred_element_type=_F32)
    z_s[...] = (zb * _sigmoid(zb)) * gn_ref[...]

    def emit(c, h, o):
        rows, sl = slice(c * chunk, (c + 1) * chunk), _head(h)
        o = o + ob_ref[0, rows, sl].astype(_F32)
        o = o * lax.rsqrt(jnp.mean(o * o, axis=-1, keepdims=True) + EPS)
        mix_s[rows, D_MODEL + h * HEAD_DIM:D_MODEL + (h + 1) * HEAD_DIM] = (
            o * z_s[rows, sl]).astype(_BF16)

    _recurrence(general, list(range(n_chunks)), p_s, q_ref, v_ref, st_ref, lb, tri2, mask, chunk - 1,
                bufs, emit, range_ref)

    out = jnp.dot(mix_s[...], wo_ref[...], preferred_element_type=_F32)
    y_ref[0] = _rmsnorm_rows(x_ref[0] + out, fg_ref[...])


def _const_spec(shape):
    return pl.BlockSpec(shape, lambda b, j: (0,) * len(shape), pipeline_mode=pl.Buffered(1))


def _w_in_spec(group, n_groups=1):
    assert group % n_groups == 0
    return pl.BlockSpec((D_MODEL, n_groups * D_MODEL), lambda b, j: (0, group // n_groups),
                        pipeline_mode=pl.Buffered(1))


def _direction_constants(reverse, chunk):
    r, c = np.indices((chunk, chunk))
    mask = ((c >= r) if reverse else (c <= r)).astype(np.float32)
    tri2 = np.concatenate([mask, mask], axis=1)
    return jnp.asarray(tri2, _BF16), jnp.asarray(mask, _F32)


def _recurrence_scratch(tile, chunk):
    n_hc = (tile // chunk) * N_HEADS
    act_bf16 = pltpu.VMEM((tile, D_MODEL), _BF16)
    return [
        pltpu.VMEM((tile, D_MODEL), _F32),
        act_bf16, act_bf16, act_bf16,
        pltpu.VMEM((tile, D_MODEL), _F32),
        pltpu.VMEM((tile // chunk, D_MODEL), _F32),
        pltpu.VMEM((n_hc, chunk, HEAD_DIM + chunk), _BF16),
        pltpu.VMEM((n_hc, HEAD_DIM + chunk, HEAD_DIM), _BF16),
    ]


def _trunks(xs, p, tile=TILE, tile_bwd=TILE_BWD):
    def attempt(chunk, general=False):
        ys, lows = zip(*(_sweeps(x, p, tile, tile_bwd, chunk, general) for x in xs))
        return ys, functools.reduce(jnp.minimum, lows) >= _HALF_DECAY_FLOOR

    def narrow(_):
        ys, in_range = attempt(CHUNK_NARROW)
        return lax.cond(in_range, lambda ys: ys, lambda _: attempt(CHUNK_NARROW, general=True)[0], ys)

    ys, in_range = attempt(CHUNK)
    return lax.cond(in_range, lambda ys: ys, narrow, ys)


def _sweeps(x, p, tile, tile_bwd, chunk, general):
    bsz, seq, d = x.shape
    assert d == D_MODEL and seq % tile == 0 and seq % tile_bwd == 0 and tile % GMLP_CHUNK == 0
    cparams = pltpu.CompilerParams(dimension_semantics=("arbitrary", "arbitrary"),
                                   vmem_limit_bytes=VMEM_LIMIT_BYTES)
    state = pltpu.VMEM((N_HEADS, HEAD_DIM, HEAD_DIM), _F32)
    o_bwd, q_act, v_act, range_b = _bwd_sweep(general, x, p, tile_bwd, chunk, cparams, state)
    y, range_f = _main_sweep(general, x, o_bwd, q_act, v_act, p, tile, chunk, cparams, state)
    return y, jnp.min(jnp.minimum(range_b, range_f))


def _range_out():
    return (pl.BlockSpec((8, HEAD_DIM), lambda b, j: (0, 0)),
            jax.ShapeDtypeStruct((8, HEAD_DIM), _F32))


def _bwd_sweep(general, x, p, tile, chunk, cparams, state):
    bsz, seq, _ = x.shape
    nt = seq // tile
    tri2_b, mask_b = _direction_constants(True, chunk)
    rev_tile = lambda b, j: (b, nt - 1 - j, 0)
    act = lambda dtype: jax.ShapeDtypeStruct((bsz, seq, D_MODEL), dtype)
    range_spec, range_shape = _range_out()
    return pl.pallas_call(
        functools.partial(_bwd_kernel, general),
        grid=(bsz, nt),
        in_specs=[
            pl.BlockSpec((1, tile, D_MODEL), rev_tile),
            _const_spec((1, D_MODEL)),
            _w_in_spec(3), _w_in_spec(5), _w_in_spec(6),
            _const_spec((2, D_MODEL)),
            _const_spec((chunk, 2 * chunk)),
            _const_spec((chunk, chunk)),
        ],
        out_specs=[pl.BlockSpec((1, tile, D_MODEL), rev_tile)] * 3 + [range_spec],
        out_shape=[act(_BF16), act(_BF16), act(_BF16), range_shape],
        scratch_shapes=_recurrence_scratch(tile, chunk) + [state],
        compiler_params=cparams,
        name=f"hgrn_bwd_sweep_c{chunk}" + ("_general" if general else ""),
    )(x, p["norm_g"], p["w_in"], p["w_in"], p["w_in"], p["lb_bwd"], tri2_b, mask_b)


def _main_sweep(general, x, o_bwd, q_act, v_act, p, tile, chunk, cparams, state):
    bsz, seq, _ = x.shape
    nt = seq // tile
    tri2_f, mask_f = _direction_constants(False, chunk)
    fwd_tile = lambda b, j: (b, j, 0)
    range_spec, range_shape = _range_out()
    return pl.pallas_call(
        functools.partial(_main_kernel, general),
        grid=(bsz, nt),
        in_specs=[
            pl.BlockSpec((1, tile, D_MODEL), fwd_tile),
            pl.BlockSpec((1, tile, D_MODEL), fwd_tile),
            pl.BlockSpec((1, tile, D_MODEL), fwd_tile),
            pl.BlockSpec((1, tile, D_MODEL), fwd_tile),
            _const_spec((1, D_MODEL)),
            _w_in_spec(0, 3), _w_in_spec(4), _w_in_spec(7),
            _const_spec((1, D_MODEL)),
            _const_spec((1, D_MODEL)),
            _const_spec((N_HEADS, GMLP_CHUNK, GMLP_CHUNK)),
            _const_spec((GMLP_CHUNK, D_MODEL)),
            _const_spec((2, D_MODEL)),
            _const_spec((1, D_MODEL)),
            _const_spec((2 * D_MODEL, D_MODEL)),
            _const_spec((1, D_MODEL)),
            _const_spec((chunk, 2 * chunk)),
            _const_spec((chunk, chunk)),
        ],
        out_specs=[pl.BlockSpec((1, tile, D_MODEL), fwd_tile), range_spec],
        out_shape=[jax.ShapeDtypeStruct((bsz, seq, D_MODEL), _F32), range_shape],
        scratch_shapes=_recurrence_scratch(tile, chunk) + [
            pltpu.VMEM((tile, D_MODEL), _F32),
            pltpu.VMEM((tile, 2 * D_MODEL), _BF16),
            state,
            pltpu.VMEM((tile, 3 * D_MODEL), _F32)],
        compiler_params=cparams,
        name=f"encoder_main_sweep_c{chunk}" + ("_general" if general else ""),
    )(x, o_bwd, q_act, v_act, p["norm_g"], p["w_in"], p["w_in"], p["w_in"], p["ln_g"], p["ln_b"], p["w_s"], p["b_s"],
      p["lb_fwd"], p["gn_g"], p["w_out"], p["final_g"], tri2_f, mask_f)


def _prepare(norm_g, w_in, ln_v_g, ln_v_b, w_s, b_s, lb_params, gn_g, w_out, final_g):
    row = lambda a: a.reshape(1, -1).astype(_F32)
    return {
        "norm_g": row(norm_g[0]),
        "w_in": w_in[0].astype(_BF16),
        "ln_g": row(ln_v_g[0]),
        "ln_b": row(ln_v_b[0]),
        "w_s": w_s[0].astype(_BF16),
        "b_s": jnp.repeat(b_s[0].T.astype(_F32), HEAD_DIM, axis=1),
        "lb_fwd": lb_params[0, :, :].astype(_F32),
        "lb_bwd": lb_params[1, :, :].astype(_F32),
        "gn_g": row(gn_g[0]),
        "w_out": w_out[0].astype(_BF16),
        "final_g": row(final_g),
    }


def kernel(x_prompt, x_sample, norm_g, w_in, ln_v_g, ln_v_b, w_s, b_s, lb_params, gn_g, w_out, final_g):
    p = _prepare(norm_g, w_in, ln_v_g, ln_v_b, w_s, b_s, lb_params, gn_g, w_out, final_g)
    return _trunks((x_prompt, x_sample), p)
```

```python
import functools

import jax
import jax.numpy as jnp
import numpy as np
from jax import lax
from jax.experimental import pallas as pl
from jax.experimental.pallas import tpu as pltpu

D_MODEL = 1024
N_HEADS = 8
HEAD_DIM = 128
GMLP_CHUNK = 128
CHUNK = 128
CHUNK_NARROW = 64
EPS = 1e-6
TILE = 512
TILE_BWD = 512
PROJ_ROWS = 256
VMEM_LIMIT_BYTES = 56 * 1024 * 1024

_BF16 = jnp.bfloat16
_F32 = jnp.float32
_LOG2E = 1.4426950408889634
_HALF_DECAY_FLOOR = 1e-26
_NT = (((1,), (1,)), ((), ()))
_TN = (((0,), (0,)), ((), ()))


def _head(h):
    return slice(h * HEAD_DIM, (h + 1) * HEAD_DIM)


def _exp(x, sign=1.0):
    return jnp.exp2(x * (sign * _LOG2E))


def _sigmoid(x):
    return 1.0 / (1.0 + _exp(x, -1.0))


def _rmsnorm_rows(x, gain):
    ms = jnp.mean(x * x, axis=-1, keepdims=True)
    return (x * lax.rsqrt(ms + EPS)) * gain


def _lower_bound(lb_pair):
    m = jnp.max(lb_pair, axis=0, keepdims=True)
    e = jnp.exp(lb_pair - m)
    return e[0:1, :] / jnp.sum(e, axis=0, keepdims=True)


def _cumsum_rows(g, tri2):
    hi = g.astype(_BF16)
    lo = (g - hi.astype(_F32)).astype(_BF16)
    g2 = jnp.concatenate([hi, lo], axis=0)
    return jnp.dot(tri2, g2, preferred_element_type=_F32)


def _split_chunk(c, f_ref, q_ref, lb, tri2, end_row, bufs):
    qt_s, kt_s, ks_s, _, eh_s, qa_s, _ = bufs
    chunk = qa_s.shape[1]
    rows = slice(c * chunk, (c + 1) * chunk)
    fg = lb + (1.0 - lb) * _sigmoid(f_ref[rows, :])
    g = jnp.log(fg)
    k = 1.0 - fg
    b = _cumsum_rows(g, tri2)
    half = 0.5 * b[end_row:end_row + 1, :]
    d = b - half
    e_half = _exp(half)
    kt = k * _exp(d, -1.0)
    kt_s[rows, :] = kt.astype(_BF16)
    ks_s[rows, :] = (kt * e_half).astype(_BF16)
    eh_s[c:c + 1, :] = e_half
    qt = q_ref[0, rows, :].astype(_F32) * _exp(d)
    qt_s[rows, :] = qt.astype(_BF16)
    qi = (qt * e_half).astype(_BF16)
    for h in range(N_HEADS):
        qa_s[c * N_HEADS + h, :, 0:HEAD_DIM] = qi[:, _head(h)]


def _general_chunk(c, f_ref, q_ref, lb, tri2, mask, end_row, bufs):
    _, _, ks_s, ed_s, _, qa_s, _ = bufs
    chunk = qa_s.shape[1]
    r0 = c * chunk
    rows = slice(r0, r0 + chunk)
    fg = lb + (1.0 - lb) * _sigmoid(f_ref[rows, :])
    g = jnp.log(fg)
    k = 1.0 - fg
    b = _cumsum_rows(g, tri2)
    total = b[end_row:end_row + 1, :]
    q = q_ref[0, rows, :].astype(_F32)
    ks_s[rows, :] = (k * _exp(total - b)).astype(_BF16)
    qi = (q * _exp(b)).astype(_BF16)
    for h in range(N_HEADS):
        qa_s[c * N_HEADS + h, :, 0:HEAD_DIM] = qi[:, _head(h)]
    ed_s[rows, :] = b
    f_ref[rows, :] = k
    col = lax.broadcasted_iota(jnp.int32, (chunk, chunk), 1)

    def one_key_row(s, acc):
        b_s = ed_s[pl.ds(r0 + s, 1), :]
        k_s = f_ref[pl.ds(r0 + s, 1), :]
        w = q * _exp(jnp.minimum(b - b_s, 0.0)) * k_s
        return tuple(a + jnp.where(col == s, jnp.sum(w[:, _head(h)], axis=-1, keepdims=True), 0.0)
                     for h, a in enumerate(acc))

    zeros = tuple(jnp.zeros((chunk, chunk), _F32) for _ in range(N_HEADS))
    acc = lax.fori_loop(0, chunk, one_key_row, zeros)
    for h in range(N_HEADS):
        qa_s[c * N_HEADS + h, :, HEAD_DIM:HEAD_DIM + chunk] = jnp.where(mask, acc[h], 0.0).astype(_BF16)


def _recurrence(general, chunk_order, f_ref, q_ref, v_ref, st_ref, lb, tri2, mask, end_row, bufs, emit,
                range_ref):
    eh_s = bufs[4]
    low = jnp.min(eh_s[...], axis=0, keepdims=True)
    low = functools.reduce(jnp.minimum, [low[:, _head(h)] for h in range(N_HEADS)])
    range_ref[...] = jnp.minimum(range_ref[...], low)
    if general:
        for c in sorted(chunk_order):
            _general_chunk(c, f_ref, q_ref, lb, tri2, mask, end_row, bufs)
    for c in chunk_order:
        _state_step(c, bufs, v_ref, st_ref, split_scores_mask=None if general else mask)
    for c in sorted(chunk_order):
        for h in range(N_HEADS):
            emit(c, h, _chunk_output(c, h, bufs))


def _state_step(c, bufs, v_ref, st_ref, split_scores_mask=None):
    qt_s, kt_s, ks_s, _, eh_s, qa_s, sv_s = bufs
    chunk = qa_s.shape[1]
    rows = slice(c * chunk, (c + 1) * chunk)
    e_half = eh_s[c:c + 1, :]
    decay = e_half * e_half
    for h in range(N_HEADS):
        sl = _head(h)
        i = c * N_HEADS + h
        if split_scores_mask is not None:
            scores = lax.dot_general(qt_s[rows, sl], kt_s[rows, sl], _NT, preferred_element_type=_F32)
            qa_s[i, :, HEAD_DIM:HEAD_DIM + chunk] = jnp.where(split_scores_mask, scores, 0.0).astype(_BF16)
        v = v_ref[0, rows, sl]
        upd = lax.dot_general(v, ks_s[rows, sl], _TN, preferred_element_type=_F32)
        st = st_ref[h]
        sv_s[i, 0:HEAD_DIM, :] = st.T.astype(_BF16)
        sv_s[i, HEAD_DIM:HEAD_DIM + chunk, :] = v
        st_ref[h] = st * decay[:, sl] + upd


def _chunk_output(c, h, bufs):
    qa_s, sv_s = bufs[-2:]
    i = c * N_HEADS + h
    return jnp.dot(qa_s[i], sv_s[i], preferred_element_type=_F32)


def _normed_input(x_ref, ng_ref, st_ref, range_ref):
    @pl.when(pl.program_id(1) == 0)
    def _():
        st_ref[...] = jnp.zeros_like(st_ref)

    @pl.when((pl.program_id(0) == 0) & (pl.program_id(1) == 0))
    def _():
        range_ref[...] = jnp.full(range_ref.shape, jnp.inf, range_ref.dtype)

    return _rmsnorm_rows(x_ref[0], ng_ref[...]).astype(_BF16)


def _bwd_kernel(general, x_ref, ng_ref, wq_ref, wf_ref, wi_ref, lb_ref, tri2_ref, mask_ref,
                o_ref, q_ref, v_ref, range_ref,
                p_s, qt_s, kt_s, ks_s, ed_s, eh_s, qa_s, sv_s, st_ref):
    chunk = qa_s.shape[1]
    n_chunks = x_ref.shape[1] // chunk
    hb = _normed_input(x_ref, ng_ref, st_ref, range_ref)
    lb = _lower_bound(lb_ref[...])
    tri2 = tri2_ref[...]
    mask = mask_ref[...] > 0.5
    bufs = (qt_s, kt_s, ks_s, ed_s, eh_s, qa_s, sv_s)
    for r in range(x_ref.shape[1] // PROJ_ROWS):
        rows = slice(r * PROJ_ROWS, (r + 1) * PROJ_ROWS)
        p_s[rows, :] = jnp.dot(hb[rows, :], wf_ref[...], preferred_element_type=_F32)
        qp = jnp.dot(hb[rows, :], wq_ref[...], preferred_element_type=_F32)
        q_ref[0, rows, :] = (qp * _sigmoid(qp)).astype(_BF16)
        v_ref[0, rows, :] = jnp.dot(hb[rows, :], wi_ref[...], preferred_element_type=_F32).astype(_BF16)
        for c in range(r * PROJ_ROWS // chunk, (r + 1) * PROJ_ROWS // chunk):
            _split_chunk(c, p_s, q_ref, lb, tri2, 0, bufs)

    def emit(c, h, o):
        o_ref[0, c * chunk:(c + 1) * chunk, _head(h)] = o.astype(_BF16)

    _recurrence(general, list(reversed(range(n_chunks))), p_s, q_ref, v_ref, st_ref, lb, tri2, mask, 0,
                bufs, emit, range_ref)


def _main_kernel(general, x_ref, ob_ref, q_ref, v_ref, ng_ref, wa_ref, wf_ref, wz_ref, lng_ref, lnb_ref,
                 ws_ref, bs_ref, lb_ref, gn_ref, wo_ref, fg_ref, tri2_ref, mask_ref, y_ref, range_ref,
                 p_s, qt_s, kt_s, ks_s, ed_s, eh_s, qa_s, sv_s, z_s, mix_s, st_ref, pa_s):
    tile = x_ref.shape[1]
    chunk = qa_s.shape[1]
    n_chunks = tile // chunk
    hb = _normed_input(x_ref, ng_ref, st_ref, range_ref)
    lb = _lower_bound(lb_ref[...])
    tri2 = tri2_ref[...]
    mask = mask_ref[...] > 0.5
    bufs = (qt_s, kt_s, ks_s, ed_s, eh_s, qa_s, sv_s)

    for r in range(tile // PROJ_ROWS):
        r0 = r * PROJ_ROWS
        rows = slice(r0, r0 + PROJ_ROWS)
        p_s[rows, :] = jnp.dot(hb[rows, :], wf_ref[...], preferred_element_type=_F32)
        pa_s[...] = jnp.dot(hb[rows, :], wa_ref[...], preferred_element_type=_F32)
        for c in range(r0 // chunk, (r0 + PROJ_ROWS) // chunk):
            _split_chunk(c, p_s, q_ref, lb, tri2, chunk - 1, bufs)
        for n in range(PROJ_ROWS // GMLP_CHUNK):
            rs = slice(n * GMLP_CHUNK, (n + 1) * GMLP_CHUNK)
            out_rows = slice(r0 + n * GMLP_CHUNK, r0 + (n + 1) * GMLP_CHUNK)
            vv = pa_s[rs, D_MODEL:2 * D_MODEL]
            vc = vv - jnp.mean(vv, axis=-1, keepdims=True)
            vn = vc * lax.rsqrt(jnp.mean(vc * vc, axis=-1, keepdims=True) + EPS)
            vn = (vn * lng_ref[...] + lnb_ref[...]).astype(_BF16)
            for h in range(N_HEADS):
                sl = _head(h)
                s = jnp.dot(ws_ref[h], vn[:, sl], preferred_element_type=_F32) + bs_ref[:, sl]
                za = pa_s[rs, 2 * D_MODEL + h * HEAD_DIM:2 * D_MODEL + (h + 1) * HEAD_DIM]
                mix_s[out_rows, sl] = ((pa_s[rs, sl] * (za * _sigmoid(za))) * s).astype(_BF16)
        zb = jnp.dot(hb[rows, :], wz_ref[...], preferred_element_type=_F32)
        z_s[rows, :] = (zb * _sigmoid(zb)) * gn_ref[...]

    def emit(c, h, o):
        rows, sl = slice(c * chunk, (c + 1) * chunk), _head(h)
        o = o + ob_ref[0, rows, sl].astype(_F32)
        o = o * lax.rsqrt(jnp.mean(o * o, axis=-1, keepdims=True) + EPS)
        mix_s[rows, D_MODEL + h * HEAD_DIM:D_MODEL + (h + 1) * HEAD_DIM] = (
            o * z_s[rows, sl]).astype(_BF16)

    _recurrence(general, list(range(n_chunks)), p_s, q_ref, v_ref, st_ref, lb, tri2, mask, chunk - 1,
                bufs, emit, range_ref)

    out = jnp.dot(mix_s[...], wo_ref[...], preferred_element_type=_F32)
    y_ref[0] = _rmsnorm_rows(x_ref[0] + out, fg_ref[...])


def _const_spec(shape):
    return pl.BlockSpec(shape, lambda b, j: (0,) * len(shape), pipeline_mode=pl.Buffered(1))


def _w_in_spec(group, n_groups=1):
    assert group % n_groups == 0
    return pl.BlockSpec((D_MODEL, n_groups * D_MODEL), lambda b, j: (0, group // n_groups),
                        pipeline_mode=pl.Buffered(1))


def _direction_constants(reverse, chunk):
    r, c = np.indices((chunk, chunk))
    mask = ((c >= r) if reverse else (c <= r)).astype(np.float32)
    tri2 = np.concatenate([mask, mask], axis=1)
    return jnp.asarray(tri2, _BF16), jnp.asarray(mask, _F32)


def _recurrence_scratch(tile, chunk):
    n_hc = (tile // chunk) * N_HEADS
    act_bf16 = pltpu.VMEM((tile, D_MODEL), _BF16)
    return [
        pltpu.VMEM((tile, D_MODEL), _F32),
        act_bf16, act_bf16, act_bf16,
        pltpu.VMEM((tile, D_MODEL), _F32),
        pltpu.VMEM((tile // chunk, D_MODEL), _F32),
        pltpu.VMEM((n_hc, chunk, HEAD_DIM + chunk), _BF16),
        pltpu.VMEM((n_hc, HEAD_DIM + chunk, HEAD_DIM), _BF16),
    ]


def _trunks(xs, p, tile=TILE, tile_bwd=TILE_BWD):
    def attempt(chunk, general=False):
        ys, lows = zip(*(_sweeps(x, p, tile, tile_bwd, chunk, general) for x in xs))
        return ys, functools.reduce(jnp.minimum, lows) >= _HALF_DECAY_FLOOR

    def narrow(_):
        ys, in_range = attempt(CHUNK_NARROW)
        return lax.cond(in_range, lambda ys: ys, lambda _: attempt(CHUNK_NARROW, general=True)[0], ys)

    ys, in_range = attempt(CHUNK)
    return lax.cond(in_range, lambda ys: ys, narrow, ys)


def _sweeps(x, p, tile, tile_bwd, chunk, general):
    bsz, seq, d = x.shape
    assert d == D_MODEL and seq % tile == 0 and seq % tile_bwd == 0 and tile % GMLP_CHUNK == 0
    cparams = pltpu.CompilerParams(dimension_semantics=("arbitrary", "arbitrary"),
                                   vmem_limit_bytes=VMEM_LIMIT_BYTES)
    state = pltpu.VMEM((N_HEADS, HEAD_DIM, HEAD_DIM), _F32)
    o_bwd, q_act, v_act, range_b = _bwd_sweep(general, x, p, tile_bwd, chunk, cparams, state)
    y, range_f = _main_sweep(general, x, o_bwd, q_act, v_act, p, tile, chunk, cparams, state)
    return y, jnp.min(jnp.minimum(range_b, range_f))


def _range_out():
    return (pl.BlockSpec((8, HEAD_DIM), lambda b, j: (0, 0)),
            jax.ShapeDtypeStruct((8, HEAD_DIM), _F32))


def _bwd_sweep(general, x, p, tile, chunk, cparams, state):
    bsz, seq, _ = x.shape
    nt = seq // tile
    tri2_b, mask_b = _direction_constants(True, chunk)
    rev_tile = lambda b, j: (b, nt - 1 - j, 0)
    act = lambda dtype: jax.ShapeDtypeStruct((bsz, seq, D_MODEL), dtype)
    range_spec, range_shape = _range_out()
    return pl.pallas_call(
        functools.partial(_bwd_kernel, general),
        grid=(bsz, nt),
        in_specs=[
            pl.BlockSpec((1, tile, D_MODEL), rev_tile),
            _const_spec((1, D_MODEL)),
            _w_in_spec(3), _w_in_spec(5), _w_in_spec(6),
            _const_spec((2, D_MODEL)),
            _const_spec((chunk, 2 * chunk)),
            _const_spec((chunk, chunk)),
        ],
        out_specs=[pl.BlockSpec((1, tile, D_MODEL), rev_tile)] * 3 + [range_spec],
        out_shape=[act(_BF16), act(_BF16), act(_BF16), range_shape],
        scratch_shapes=_recurrence_scratch(tile, chunk) + [state],
        compiler_params=cparams,
        name=f"hgrn_bwd_sweep_c{chunk}" + ("_general" if general else ""),
    )(x, p["norm_g"], p["w_in"], p["w_in"], p["w_in"], p["lb_bwd"], tri2_b, mask_b)


def _main_sweep(general, x, o_bwd, q_act, v_act, p, tile, chunk, cparams, state):
    bsz, seq, _ = x.shape
    nt = seq // tile
    tri2_f, mask_f = _direction_constants(False, chunk)
    fwd_tile = lambda b, j: (b, j, 0)
    range_spec, range_shape = _range_out()
    return pl.pallas_call(
        functools.partial(_main_kernel, general),
        grid=(bsz, nt),
        in_specs=[
            pl.BlockSpec((1, tile, D_MODEL), fwd_tile),
            pl.BlockSpec((1, tile, D_MODEL), fwd_tile),
            pl.BlockSpec((1, tile, D_MODEL), fwd_tile),
            pl.BlockSpec((1, tile, D_MODEL), fwd_tile),
            _const_spec((1, D_MODEL)),
            _w_in_spec(0, 3), _w_in_spec(4), _w_in_spec(7),
            _const_spec((1, D_MODEL)),
            _const_spec((1, D_MODEL)),
            _const_spec((N_HEADS, GMLP_CHUNK, GMLP_CHUNK)),
            _const_spec((GMLP_CHUNK, D_MODEL)),
            _const_spec((2, D_MODEL)),
            _const_spec((1, D_MODEL)),
            _const_spec((2 * D_MODEL, D_MODEL)),
            _const_spec((1, D_MODEL)),
            _const_spec((chunk, 2 * chunk)),
            _const_spec((chunk, chunk)),
        ],
        out_specs=[pl.BlockSpec((1, tile, D_MODEL), fwd_tile), range_spec],
        out_shape=[jax.ShapeDtypeStruct((bsz, seq, D_MODEL), _F32), range_shape],
        scratch_shapes=_recurrence_scratch(tile, chunk) + [
            pltpu.VMEM((tile, D_MODEL), _F32),
            pltpu.VMEM((tile, 2 * D_MODEL), _BF16),
            state,
            pltpu.VMEM((PROJ_ROWS, 3 * D_MODEL), _F32)],
        compiler_params=cparams,
        name=f"encoder_main_sweep_c{chunk}" + ("_general" if general else ""),
    )(x, o_bwd, q_act, v_act, p["norm_g"], p["w_in"], p["w_in"], p["w_in"], p["ln_g"], p["ln_b"], p["w_s"], p["b_s"],
      p["lb_fwd"], p["gn_g"], p["w_out"], p["final_g"], tri2_f, mask_f)


def _prepare(norm_g, w_in, ln_v_g, ln_v_b, w_s, b_s, lb_params, gn_g, w_out, final_g):
    row = lambda a: a.reshape(1, -1).astype(_F32)
    return {
        "norm_g": row(norm_g[0]),
        "w_in": w_in[0].astype(_BF16),
        "ln_g": row(ln_v_g[0]),
        "ln_b": row(ln_v_b[0]),
        "w_s": w_s[0].astype(_BF16),
        "b_s": jnp.repeat(b_s[0].T.astype(_F32), HEAD_DIM, axis=1),
        "lb_fwd": lb_params[0, :, :].astype(_F32),
        "lb_bwd": lb_params[1, :, :].astype(_F32),
        "gn_g": row(gn_g[0]),
        "w_out": w_out[0].astype(_BF16),
        "final_g": row(final_g),
    }


def kernel(x_prompt, x_sample, norm_g, w_in, ln_v_g, ln_v_b, w_s, b_s, lb_params, gn_g, w_out, final_g):
    p = _prepare(norm_g, w_in, ln_v_g, ln_v_b, w_s, b_s, lb_params, gn_g, w_out, final_g)
    return _trunks((x_prompt, x_sample), p)
```

```python
import functools

import jax
import jax.numpy as jnp
import numpy as np
from jax import lax
from jax.experimental import pallas as pl
from jax.experimental.pallas import tpu as pltpu

D_MODEL = 1024
N_HEADS = 8
HEAD_DIM = 128
GMLP_CHUNK = 128
CHUNK = 128
CHUNK_NARROW = 64
EPS = 1e-6
TILE = 512
TILE_BWD = 1024
PROJ_ROWS = 256
VMEM_LIMIT_BYTES = 56 * 1024 * 1024

_BF16 = jnp.bfloat16
_F32 = jnp.float32
_LOG2E = 1.4426950408889634
_HALF_DECAY_FLOOR = 1e-26
_NT = (((1,), (1,)), ((), ()))
_TN = (((0,), (0,)), ((), ()))


def _head(h):
    return slice(h * HEAD_DIM, (h + 1) * HEAD_DIM)


def _exp(x, sign=1.0):
    return jnp.exp2(x * (sign * _LOG2E))


def _sigmoid(x):
    return 1.0 / (1.0 + _exp(x, -1.0))


def _rmsnorm_rows(x, gain):
    ms = jnp.mean(x * x, axis=-1, keepdims=True)
    return (x * lax.rsqrt(ms + EPS)) * gain


def _lower_bound(lb_pair):
    m = jnp.max(lb_pair, axis=0, keepdims=True)
    e = jnp.exp(lb_pair - m)
    return e[0:1, :] / jnp.sum(e, axis=0, keepdims=True)


def _cumsum_rows(g, tri2):
    hi = g.astype(_BF16)
    lo = (g - hi.astype(_F32)).astype(_BF16)
    g2 = jnp.concatenate([hi, lo], axis=0)
    return jnp.dot(tri2, g2, preferred_element_type=_F32)


def _split_chunk(c, f_ref, q_ref, lb, tri2, end_row, bufs):
    qt_s, kt_s, ks_s, _, eh_s, qa_s, _ = bufs
    chunk = qa_s.shape[1]
    rows = slice(c * chunk, (c + 1) * chunk)
    fg = lb + (1.0 - lb) * _sigmoid(f_ref[rows, :])
    g = jnp.log(fg)
    k = 1.0 - fg
    b = _cumsum_rows(g, tri2)
    half = 0.5 * b[end_row:end_row + 1, :]
    d = b - half
    e_half = _exp(half)
    kt = k * _exp(d, -1.0)
    kt_s[rows, :] = kt.astype(_BF16)
    ks_s[rows, :] = (kt * e_half).astype(_BF16)
    eh_s[c:c + 1, :] = e_half
    qt = q_ref[0, rows, :].astype(_F32) * _exp(d)
    qt_s[rows, :] = qt.astype(_BF16)
    qi = (qt * e_half).astype(_BF16)
    for h in range(N_HEADS):
        qa_s[c * N_HEADS + h, :, 0:HEAD_DIM] = qi[:, _head(h)]


def _general_chunk(c, f_ref, q_ref, lb, tri2, mask, end_row, bufs):
    _, _, ks_s, ed_s, _, qa_s, _ = bufs
    chunk = qa_s.shape[1]
    r0 = c * chunk
    rows = slice(r0, r0 + chunk)
    fg = lb + (1.0 - lb) * _sigmoid(f_ref[rows, :])
    g = jnp.log(fg)
    k = 1.0 - fg
    b = _cumsum_rows(g, tri2)
    total = b[end_row:end_row + 1, :]
    q = q_ref[0, rows, :].astype(_F32)
    ks_s[rows, :] = (k * _exp(total - b)).astype(_BF16)
    qi = (q * _exp(b)).astype(_BF16)
    for h in range(N_HEADS):
        qa_s[c * N_HEADS + h, :, 0:HEAD_DIM] = qi[:, _head(h)]
    ed_s[rows, :] = b
    f_ref[rows, :] = k
    col = lax.broadcasted_iota(jnp.int32, (chunk, chunk), 1)

    def one_key_row(s, acc):
        b_s = ed_s[pl.ds(r0 + s, 1), :]
        k_s = f_ref[pl.ds(r0 + s, 1), :]
        w = q * _exp(jnp.minimum(b - b_s, 0.0)) * k_s
        return tuple(a + jnp.where(col == s, jnp.sum(w[:, _head(h)], axis=-1, keepdims=True), 0.0)
                     for h, a in enumerate(acc))

    zeros = tuple(jnp.zeros((chunk, chunk), _F32) for _ in range(N_HEADS))
    acc = lax.fori_loop(0, chunk, one_key_row, zeros)
    for h in range(N_HEADS):
        qa_s[c * N_HEADS + h, :, HEAD_DIM:HEAD_DIM + chunk] = jnp.where(mask, acc[h], 0.0).astype(_BF16)


def _accumulate_range(bufs, range_ref):
    eh_s = bufs[4]
    low = jnp.min(eh_s[...], axis=0, keepdims=True)
    low = functools.reduce(jnp.minimum, [low[:, _head(h)] for h in range(N_HEADS)])
    range_ref[...] = jnp.minimum(range_ref[...], low)


def _recurrence(general, chunk_order, f_ref, q_ref, v_ref, st_ref, lb, tri2, mask, end_row, bufs, emit):
    if general:
        for c in sorted(chunk_order):
            _general_chunk(c, f_ref, q_ref, lb, tri2, mask, end_row, bufs)
    for c in chunk_order:
        _state_step(c, bufs, v_ref, st_ref, split_scores_mask=None if general else mask)
    for c in sorted(chunk_order):
        for h in range(N_HEADS):
            emit(c, h, _chunk_output(c, h, bufs))


def _state_step(c, bufs, v_ref, st_ref, split_scores_mask=None):
    qt_s, kt_s, ks_s, _, eh_s, qa_s, sv_s = bufs
    chunk = qa_s.shape[1]
    rows = slice(c * chunk, (c + 1) * chunk)
    e_half = eh_s[c:c + 1, :]
    decay = e_half * e_half
    for h in range(N_HEADS):
        sl = _head(h)
        i = c * N_HEADS + h
        if split_scores_mask is not None:
            scores = lax.dot_general(qt_s[rows, sl], kt_s[rows, sl], _NT, preferred_element_type=_F32)
            qa_s[i, :, HEAD_DIM:HEAD_DIM + chunk] = jnp.where(split_scores_mask, scores, 0.0).astype(_BF16)
        v = v_ref[0, rows, sl]
        upd = lax.dot_general(v, ks_s[rows, sl], _TN, preferred_element_type=_F32)
        st = st_ref[h]
        sv_s[i, 0:HEAD_DIM, :] = st.T.astype(_BF16)
        sv_s[i, HEAD_DIM:HEAD_DIM + chunk, :] = v
        st_ref[h] = st * decay[:, sl] + upd


def _chunk_output(c, h, bufs):
    qa_s, sv_s = bufs[-2:]
    i = c * N_HEADS + h
    return jnp.dot(qa_s[i], sv_s[i], preferred_element_type=_F32)


def _start_step(st_ref, range_ref):
    @pl.when(pl.program_id(1) == 0)
    def _():
        st_ref[...] = jnp.zeros_like(st_ref)

    @pl.when((pl.program_id(0) == 0) & (pl.program_id(1) == 0))
    def _():
        range_ref[...] = jnp.full(range_ref.shape, jnp.inf, range_ref.dtype)


def _normed_rows(x_ref, rows, ng_ref):
    return _rmsnorm_rows(x_ref[0, rows, :], ng_ref[...]).astype(_BF16)


def _bwd_kernel(general, x_ref, ng_ref, wq_ref, wf_ref, wi_ref, lb_ref, tri2_ref, mask_ref,
                o_ref, q_ref, v_ref, range_ref,
                p_s, qt_s, kt_s, ks_s, ed_s, eh_s, qa_s, sv_s, st_ref):
    chunk = qa_s.shape[1]
    n_chunks = x_ref.shape[1] // chunk
    _start_step(st_ref, range_ref)
    lb = _lower_bound(lb_ref[...])
    tri2 = tri2_ref[...]
    mask = mask_ref[...] > 0.5
    bufs = (qt_s, kt_s, ks_s, ed_s, eh_s, qa_s, sv_s)

    def emit(c, h, o):
        o_ref[0, c * chunk:(c + 1) * chunk, _head(h)] = o.astype(_BF16)

    for r in range(x_ref.shape[1] // PROJ_ROWS):
        rows = slice(r * PROJ_ROWS, (r + 1) * PROJ_ROWS)
        hb = _normed_rows(x_ref, rows, ng_ref)
        p_s[rows, :] = jnp.dot(hb, wf_ref[...], preferred_element_type=_F32)
        qp = jnp.dot(hb, wq_ref[...], preferred_element_type=_F32)
        q_ref[0, rows, :] = (qp * _sigmoid(qp)).astype(_BF16)
        v_ref[0, rows, :] = jnp.dot(hb, wi_ref[...], preferred_element_type=_F32).astype(_BF16)
        for c in range(r * PROJ_ROWS // chunk, (r + 1) * PROJ_ROWS // chunk):
            _split_chunk(c, p_s, q_ref, lb, tri2, 0, bufs)
    _accumulate_range(bufs, range_ref)
    _recurrence(general, list(reversed(range(n_chunks))), p_s, q_ref, v_ref, st_ref, lb, tri2, mask, 0,
                bufs, emit)


def _main_kernel(general, x_ref, ob_ref, q_ref, v_ref, ng_ref, wa_ref, wf_ref, wz_ref, lng_ref, lnb_ref,
                 ws_ref, bs_ref, lb_ref, gn_ref, wo_ref, fg_ref, tri2_ref, mask_ref, y_ref, range_ref,
                 p_s, qt_s, kt_s, ks_s, ed_s, eh_s, qa_s, sv_s, z_s, mix_s, st_ref, pa_s):
    tile = x_ref.shape[1]
    chunk = qa_s.shape[1]
    n_chunks = tile // chunk
    _start_step(st_ref, range_ref)
    lb = _lower_bound(lb_ref[...])
    tri2 = tri2_ref[...]
    mask = mask_ref[...] > 0.5
    bufs = (qt_s, kt_s, ks_s, ed_s, eh_s, qa_s, sv_s)

    def emit(c, h, o):
        rows, sl = slice(c * chunk, (c + 1) * chunk), _head(h)
        o = o + ob_ref[0, rows, sl].astype(_F32)
        o = o * lax.rsqrt(jnp.mean(o * o, axis=-1, keepdims=True) + EPS)
        mix_s[rows, D_MODEL + h * HEAD_DIM:D_MODEL + (h + 1) * HEAD_DIM] = (
            o * z_s[rows, sl]).astype(_BF16)

    for r in range(tile // PROJ_ROWS):
        r0 = r * PROJ_ROWS
        rows = slice(r0, r0 + PROJ_ROWS)
        hb = _normed_rows(x_ref, rows, ng_ref)
        p_s[rows, :] = jnp.dot(hb, wf_ref[...], preferred_element_type=_F32)
        pa_s[...] = jnp.dot(hb, wa_ref[...], preferred_element_type=_F32)
        for c in range(r0 // chunk, (r0 + PROJ_ROWS) // chunk):
            _split_chunk(c, p_s, q_ref, lb, tri2, chunk - 1, bufs)
        for n in range(PROJ_ROWS // GMLP_CHUNK):
            rs = slice(n * GMLP_CHUNK, (n + 1) * GMLP_CHUNK)
            out_rows = slice(r0 + n * GMLP_CHUNK, r0 + (n + 1) * GMLP_CHUNK)
            vv = pa_s[rs, D_MODEL:2 * D_MODEL]
            vc = vv - jnp.mean(vv, axis=-1, keepdims=True)
            vn = vc * lax.rsqrt(jnp.mean(vc * vc, axis=-1, keepdims=True) + EPS)
            vn = (vn * lng_ref[...] + lnb_ref[...]).astype(_BF16)
            for h in range(N_HEADS):
                sl = _head(h)
                s = jnp.dot(ws_ref[h], vn[:, sl], preferred_element_type=_F32) + bs_ref[:, sl]
                za = pa_s[rs, 2 * D_MODEL + h * HEAD_DIM:2 * D_MODEL + (h + 1) * HEAD_DIM]
                mix_s[out_rows, sl] = ((pa_s[rs, sl] * (za * _sigmoid(za))) * s).astype(_BF16)
        zb = jnp.dot(hb, wz_ref[...], preferred_element_type=_F32)
        z_s[rows, :] = (zb * _sigmoid(zb)) * gn_ref[...]
    _accumulate_range(bufs, range_ref)
    _recurrence(general, list(range(n_chunks)), p_s, q_ref, v_ref, st_ref, lb, tri2, mask, chunk - 1,
                bufs, emit)

    out = jnp.dot(mix_s[...], wo_ref[...], preferred_element_type=_F32)
    y_ref[0] = _rmsnorm_rows(x_ref[0] + out, fg_ref[...])


def _const_spec(shape):
    return pl.BlockSpec(shape, lambda b, j: (0,) * len(shape), pipeline_mode=pl.Buffered(1))


def _w_in_spec(group, n_groups=1):
    assert group % n_groups == 0
    return pl.BlockSpec((D_MODEL, n_groups * D_MODEL), lambda b, j: (0, group // n_groups),
                        pipeline_mode=pl.Buffered(1))


def _direction_constants(reverse, chunk):
    r, c = np.indices((chunk, chunk))
    mask = ((c >= r) if reverse else (c <= r)).astype(np.float32)
    tri2 = np.concatenate([mask, mask], axis=1)
    return jnp.asarray(tri2, _BF16), jnp.asarray(mask, _F32)


def _recurrence_scratch(tile, chunk):
    n_hc = (tile // chunk) * N_HEADS
    act_bf16 = pltpu.VMEM((tile, D_MODEL), _BF16)
    return [
        pltpu.VMEM((tile, D_MODEL), _F32),
        act_bf16, act_bf16, act_bf16,
        pltpu.VMEM((tile, D_MODEL), _F32),
        pltpu.VMEM((tile // chunk, D_MODEL), _F32),
        pltpu.VMEM((n_hc, chunk, HEAD_DIM + chunk), _BF16),
        pltpu.VMEM((n_hc, HEAD_DIM + chunk, HEAD_DIM), _BF16),
    ]


def _trunks(xs, p, tile=TILE, tile_bwd=TILE_BWD):
    def attempt(chunk, general=False):
        ys, lows = zip(*(_sweeps(x, p, tile, tile_bwd, chunk, general) for x in xs))
        return ys, functools.reduce(jnp.minimum, lows) >= _HALF_DECAY_FLOOR

    def narrow(_):
        ys, in_range = attempt(CHUNK_NARROW)
        return lax.cond(in_range, lambda ys: ys, lambda _: attempt(CHUNK_NARROW, general=True)[0], ys)

    ys, in_range = attempt(CHUNK)
    return lax.cond(in_range, lambda ys: ys, narrow, ys)


def _sweeps(x, p, tile, tile_bwd, chunk, general):
    bsz, seq, d = x.shape
    assert d == D_MODEL and seq % tile == 0 and seq % tile_bwd == 0 and tile % GMLP_CHUNK == 0
    cparams = pltpu.CompilerParams(dimension_semantics=("arbitrary", "arbitrary"),
                                   vmem_limit_bytes=VMEM_LIMIT_BYTES)
    state = pltpu.VMEM((N_HEADS, HEAD_DIM, HEAD_DIM), _F32)
    o_bwd, q_act, v_act, range_b = _bwd_sweep(general, x, p, tile_bwd, chunk, cparams, state)
    y, range_f = _main_sweep(general, x, o_bwd, q_act, v_act, p, tile, chunk, cparams, state)
    return y, jnp.min(jnp.minimum(range_b, range_f))


def _range_out():
    return (pl.BlockSpec((8, HEAD_DIM), lambda b, j: (0, 0)),
            jax.ShapeDtypeStruct((8, HEAD_DIM), _F32))


def _bwd_sweep(general, x, p, tile, chunk, cparams, state):
    bsz, seq, _ = x.shape
    nt = seq // tile
    tri2_b, mask_b = _direction_constants(True, chunk)
    rev_tile = lambda b, j: (b, nt - 1 - j, 0)
    act = lambda dtype: jax.ShapeDtypeStruct((bsz, seq, D_MODEL), dtype)
    range_spec, range_shape = _range_out()
    return pl.pallas_call(
        functools.partial(_bwd_kernel, general),
        grid=(bsz, nt),
        in_specs=[
            pl.BlockSpec((1, tile, D_MODEL), rev_tile),
            _const_spec((1, D_MODEL)),
            _w_in_spec(3), _w_in_spec(5), _w_in_spec(6),
            _const_spec((2, D_MODEL)),
            _const_spec((chunk, 2 * chunk)),
            _const_spec((chunk, chunk)),
        ],
        out_specs=[pl.BlockSpec((1, tile, D_MODEL), rev_tile)] * 3 + [range_spec],
        out_shape=[act(_BF16), act(_BF16), act(_BF16), range_shape],
        scratch_shapes=_recurrence_scratch(tile, chunk) + [state],
        compiler_params=cparams,
        name=f"hgrn_bwd_sweep_c{chunk}" + ("_general" if general else ""),
    )(x, p["norm_g"], p["w_in"], p["w_in"], p["w_in"], p["lb_bwd"], tri2_b, mask_b)


def _main_sweep(general, x, o_bwd, q_act, v_act, p, tile, chunk, cparams, state):
    bsz, seq, _ = x.shape
    nt = seq // tile
    tri2_f, mask_f = _direction_constants(False, chunk)
    fwd_tile = lambda b, j: (b, j, 0)
    range_spec, range_shape = _range_out()
    return pl.pallas_call(
        functools.partial(_main_kernel, general),
        grid=(bsz, nt),
        in_specs=[
            pl.BlockSpec((1, tile, D_MODEL), fwd_tile),
            pl.BlockSpec((1, tile, D_MODEL), fwd_tile),
            pl.BlockSpec((1, tile, D_MODEL), fwd_tile),
            pl.BlockSpec((1, tile, D_MODEL), fwd_tile),
            _const_spec((1, D_MODEL)),
            _w_in_spec(0, 3), _w_in_spec(4), _w_in_spec(7),
            _const_spec((1, D_MODEL)),
            _const_spec((1, D_MODEL)),
            _const_spec((N_HEADS, GMLP_CHUNK, GMLP_CHUNK)),
            _const_spec((GMLP_CHUNK, D_MODEL)),
            _const_spec((2, D_MODEL)),
            _const_spec((1, D_MODEL)),
            _const_spec((2 * D_MODEL, D_MODEL)),
            _const_spec((1, D_MODEL)),
            _const_spec((chunk, 2 * chunk)),
            _const_spec((chunk, chunk)),
        ],
        out_specs=[pl.BlockSpec((1, tile, D_MODEL), fwd_tile), range_spec],
        out_shape=[jax.ShapeDtypeStruct((bsz, seq, D_MODEL), _F32), range_shape],
        scratch_shapes=_recurrence_scratch(tile, chunk) + [
            pltpu.VMEM((tile, D_MODEL), _F32),
            pltpu.VMEM((tile, 2 * D_MODEL), _BF16),
            state,
            pltpu.VMEM((PROJ_ROWS, 3 * D_MODEL), _F32)],
        compiler_params=cparams,
        name=f"encoder_main_sweep_c{chunk}" + ("_general" if general else ""),
    )(x, o_bwd, q_act, v_act, p["norm_g"], p["w_in"], p["w_in"], p["w_in"], p["ln_g"], p["ln_b"], p["w_s"], p["b_s"],
      p["lb_fwd"], p["gn_g"], p["w_out"], p["final_g"], tri2_f, mask_f)


def _prepare(norm_g, w_in, ln_v_g, ln_v_b, w_s, b_s, lb_params, gn_g, w_out, final_g):
    row = lambda a: a.reshape(1, -1).astype(_F32)
    return {
        "norm_g": row(norm_g[0]),
        "w_in": w_in[0].astype(_BF16),
        "ln_g": row(ln_v_g[0]),
        "ln_b": row(ln_v_b[0]),
        "w_s": w_s[0].astype(_BF16),
        "b_s": jnp.repeat(b_s[0].T.astype(_F32), HEAD_DIM, axis=1),
        "lb_fwd": lb_params[0, :, :].astype(_F32),
        "lb_bwd": lb_params[1, :, :].astype(_F32),
        "gn_g": row(gn_g[0]),
        "w_out": w_out[0].astype(_BF16),
        "final_g": row(final_g),
    }


def kernel(x_prompt, x_sample, norm_g, w_in, ln_v_g, ln_v_b, w_s, b_s, lb_params, gn_g, w_out, final_g):
    p = _prepare(norm_g, w_in, ln_v_g, ln_v_b, w_s, b_s, lb_params, gn_g, w_out, final_g)
    return _trunks((x_prompt, x_sample), p)
```

```python
import functools

import jax
import jax.numpy as jnp
import numpy as np
from jax import lax
from jax.experimental import pallas as pl
from jax.experimental.pallas import tpu as pltpu

D_MODEL = 1024
N_HEADS = 8
HEAD_DIM = 128
GMLP_CHUNK = 128
CHUNK = 128
CHUNK_NARROW = 64
EPS = 1e-6
TILE = 512
TILE_BWD = 1024
PROJ_ROWS = 256
VMEM_LIMIT_BYTES = 56 * 1024 * 1024

_BF16 = jnp.bfloat16
_F32 = jnp.float32
_LOG2E = 1.4426950408889634
_HALF_DECAY_FLOOR = 1e-26
_NT = (((1,), (1,)), ((), ()))
_TN = (((0,), (0,)), ((), ()))


def _head(h):
    return slice(h * HEAD_DIM, (h + 1) * HEAD_DIM)


def _exp(x, sign=1.0):
    return jnp.exp2(x * (sign * _LOG2E))


def _sigmoid(x):
    return 1.0 / (1.0 + _exp(x, -1.0))


def _rmsnorm_rows(x, gain):
    ms = jnp.mean(x * x, axis=-1, keepdims=True)
    return (x * lax.rsqrt(ms + EPS)) * gain


def _lower_bound(lb_pair):
    m = jnp.max(lb_pair, axis=0, keepdims=True)
    e = jnp.exp(lb_pair - m)
    return e[0:1, :] / jnp.sum(e, axis=0, keepdims=True)


def _cumsum_rows(g, tri2):
    hi = g.astype(_BF16)
    lo = (g - hi.astype(_F32)).astype(_BF16)
    g2 = jnp.concatenate([hi, lo], axis=0)
    return jnp.dot(tri2, g2, preferred_element_type=_F32)


def _split_chunk(c, f_ref, q_ref, lb, tri2, end_row, bufs):
    qt_s, kt_s, ks_s, _, eh_s, qa_s, _ = bufs
    chunk = qa_s.shape[1]
    rows = slice(c * chunk, (c + 1) * chunk)
    fg = lb + (1.0 - lb) * _sigmoid(f_ref[rows, :])
    g = jnp.log(fg)
    k = 1.0 - fg
    b = _cumsum_rows(g, tri2)
    half = 0.5 * b[end_row:end_row + 1, :]
    d = b - half
    e_half = _exp(half)
    kt = k * _exp(d, -1.0)
    kt_s[rows, :] = kt.astype(_BF16)
    ks_s[rows, :] = (kt * e_half).astype(_BF16)
    eh_s[c:c + 1, :] = e_half
    qt = q_ref[0, rows, :].astype(_F32) * _exp(d)
    qt_s[rows, :] = qt.astype(_BF16)
    qi = (qt * e_half).astype(_BF16)
    for h in range(N_HEADS):
        qa_s[c * N_HEADS + h, :, 0:HEAD_DIM] = qi[:, _head(h)]


def _general_chunk(c, f_ref, q_ref, lb, tri2, mask, end_row, bufs):
    _, _, ks_s, ed_s, _, qa_s, _ = bufs
    chunk = qa_s.shape[1]
    r0 = c * chunk
    rows = slice(r0, r0 + chunk)
    fg = lb + (1.0 - lb) * _sigmoid(f_ref[rows, :])
    g = jnp.log(fg)
    k = 1.0 - fg
    b = _cumsum_rows(g, tri2)
    total = b[end_row:end_row + 1, :]
    q = q_ref[0, rows, :].astype(_F32)
    ks_s[rows, :] = (k * _exp(total - b)).astype(_BF16)
    qi = (q * _exp(b)).astype(_BF16)
    for h in range(N_HEADS):
        qa_s[c * N_HEADS + h, :, 0:HEAD_DIM] = qi[:, _head(h)]
    ed_s[rows, :] = b
    f_ref[rows, :] = k
    col = lax.broadcasted_iota(jnp.int32, (chunk, chunk), 1)

    def one_key_row(s, acc):
        b_s = ed_s[pl.ds(r0 + s, 1), :]
        k_s = f_ref[pl.ds(r0 + s, 1), :]
        w = q * _exp(jnp.minimum(b - b_s, 0.0)) * k_s
        return tuple(a + jnp.where(col == s, jnp.sum(w[:, _head(h)], axis=-1, keepdims=True), 0.0)
                     for h, a in enumerate(acc))

    zeros = tuple(jnp.zeros((chunk, chunk), _F32) for _ in range(N_HEADS))
    acc = lax.fori_loop(0, chunk, one_key_row, zeros)
    for h in range(N_HEADS):
        qa_s[c * N_HEADS + h, :, HEAD_DIM:HEAD_DIM + chunk] = jnp.where(mask, acc[h], 0.0).astype(_BF16)


def _accumulate_range(bufs, range_ref):
    eh_s = bufs[4]
    low = jnp.min(eh_s[...], axis=0, keepdims=True)
    low = functools.reduce(jnp.minimum, [low[:, _head(h)] for h in range(N_HEADS)])
    range_ref[...] = jnp.minimum(range_ref[...], low)


def _recurrence(general, chunk_order, f_ref, q_ref, v_ref, st_ref, lb, tri2, mask, end_row, bufs, emit):
    if general:
        for c in sorted(chunk_order):
            _general_chunk(c, f_ref, q_ref, lb, tri2, mask, end_row, bufs)
    for c in chunk_order:
        _state_step(c, bufs, v_ref, st_ref, split_scores_mask=None if general else mask)
    for c in sorted(chunk_order):
        for h in range(N_HEADS):
            emit(c, h, _chunk_output(c, h, bufs))


def _state_step(c, bufs, v_ref, st_ref, split_scores_mask=None):
    qt_s, kt_s, ks_s, _, eh_s, qa_s, sv_s = bufs
    chunk = qa_s.shape[1]
    rows = slice(c * chunk, (c + 1) * chunk)
    e_half = eh_s[c:c + 1, :]
    decay = e_half * e_half
    for h in range(N_HEADS):
        sl = _head(h)
        i = c * N_HEADS + h
        if split_scores_mask is not None:
            scores = lax.dot_general(qt_s[rows, sl], kt_s[rows, sl], _NT, preferred_element_type=_F32)
            qa_s[i, :, HEAD_DIM:HEAD_DIM + chunk] = jnp.where(split_scores_mask, scores, 0.0).astype(_BF16)
        v = v_ref[0, rows, sl]
        upd = lax.dot_general(v, ks_s[rows, sl], _TN, preferred_element_type=_F32)
        st = st_ref[h]
        sv_s[i, 0:HEAD_DIM, :] = st.T.astype(_BF16)
        sv_s[i, HEAD_DIM:HEAD_DIM + chunk, :] = v
        st_ref[h] = st * decay[:, sl] + upd


def _chunk_output(c, h, bufs):
    qa_s, sv_s = bufs[-2:]
    i = c * N_HEADS + h
    return jnp.dot(qa_s[i], sv_s[i], preferred_element_type=_F32)


def _start_step(st_ref, range_ref):
    @pl.when(pl.program_id(1) == 0)
    def _():
        st_ref[...] = jnp.zeros_like(st_ref)

    @pl.when((pl.program_id(0) == 0) & (pl.program_id(1) == 0))
    def _():
        range_ref[...] = jnp.full(range_ref.shape, jnp.inf, range_ref.dtype)


def _normed_rows(x_ref, rows, ng_ref):
    return _rmsnorm_rows(x_ref[0, rows, :], ng_ref[...]).astype(_BF16)


def _bwd_kernel(general, x_ref, ng_ref, wq_ref, wf_ref, wi_ref, lb_ref, tri2_ref, mask_ref,
                o_ref, q_ref, v_ref, range_ref,
                p_s, qt_s, kt_s, ks_s, ed_s, eh_s, qa_s, sv_s, st_ref):
    chunk = qa_s.shape[1]
    n_chunks = x_ref.shape[1] // chunk
    _start_step(st_ref, range_ref)
    lb = _lower_bound(lb_ref[...])
    tri2 = tri2_ref[...]
    mask = mask_ref[...] > 0.5
    bufs = (qt_s, kt_s, ks_s, ed_s, eh_s, qa_s, sv_s)

    def emit(c, h, o):
        o_ref[0, c * chunk:(c + 1) * chunk, _head(h)] = o.astype(_BF16)

    n_blocks = x_ref.shape[1] // PROJ_ROWS
    group = max(n_blocks // 2, 1)
    for g in reversed(range(n_blocks // group)):
        for r in reversed(range(g * group, (g + 1) * group)):
            rows = slice(r * PROJ_ROWS, (r + 1) * PROJ_ROWS)
            hb = _normed_rows(x_ref, rows, ng_ref)
            p_s[rows, :] = jnp.dot(hb, wf_ref[...], preferred_element_type=_F32)
            qp = jnp.dot(hb, wq_ref[...], preferred_element_type=_F32)
            q_ref[0, rows, :] = (qp * _sigmoid(qp)).astype(_BF16)
            v_ref[0, rows, :] = jnp.dot(hb, wi_ref[...], preferred_element_type=_F32).astype(_BF16)
            for c in range(r * PROJ_ROWS // chunk, (r + 1) * PROJ_ROWS // chunk):
                _split_chunk(c, p_s, q_ref, lb, tri2, 0, bufs)
        chunks = range(g * group * PROJ_ROWS // chunk, (g + 1) * group * PROJ_ROWS // chunk)
        _recurrence(general, list(reversed(chunks)), p_s, q_ref, v_ref, st_ref, lb, tri2, mask, 0,
                    bufs, emit)
    _accumulate_range(bufs, range_ref)


def _main_kernel(general, x_ref, ob_ref, q_ref, v_ref, ng_ref, wa_ref, wf_ref, wz_ref, lng_ref, lnb_ref,
                 ws_ref, bs_ref, lb_ref, gn_ref, wo_ref, fg_ref, tri2_ref, mask_ref, y_ref, range_ref,
                 p_s, qt_s, kt_s, ks_s, ed_s, eh_s, qa_s, sv_s, z_s, mix_s, st_ref, pa_s):
    tile = x_ref.shape[1]
    chunk = qa_s.shape[1]
    n_chunks = tile // chunk
    _start_step(st_ref, range_ref)
    lb = _lower_bound(lb_ref[...])
    tri2 = tri2_ref[...]
    mask = mask_ref[...] > 0.5
    bufs = (qt_s, kt_s, ks_s, ed_s, eh_s, qa_s, sv_s)

    def emit(c, h, o):
        rows, sl = slice(c * chunk, (c + 1) * chunk), _head(h)
        o = o + ob_ref[0, rows, sl].astype(_F32)
        o = o * lax.rsqrt(jnp.mean(o * o, axis=-1, keepdims=True) + EPS)
        mix_s[rows, D_MODEL + h * HEAD_DIM:D_MODEL + (h + 1) * HEAD_DIM] = (
            o * z_s[rows, sl]).astype(_BF16)

    for r in range(tile // PROJ_ROWS):
        r0 = r * PROJ_ROWS
        rows = slice(r0, r0 + PROJ_ROWS)
        hb = _normed_rows(x_ref, rows, ng_ref)
        p_s[rows, :] = jnp.dot(hb, wf_ref[...], preferred_element_type=_F32)
        pa_s[...] = jnp.dot(hb, wa_ref[...], preferred_element_type=_F32)
        for c in range(r0 // chunk, (r0 + PROJ_ROWS) // chunk):
            _split_chunk(c, p_s, q_ref, lb, tri2, chunk - 1, bufs)
        for n in range(PROJ_ROWS // GMLP_CHUNK):
            rs = slice(n * GMLP_CHUNK, (n + 1) * GMLP_CHUNK)
            out_rows = slice(r0 + n * GMLP_CHUNK, r0 + (n + 1) * GMLP_CHUNK)
            vv = pa_s[rs, D_MODEL:2 * D_MODEL]
            vc = vv - jnp.mean(vv, axis=-1, keepdims=True)
            vn = vc * lax.rsqrt(jnp.mean(vc * vc, axis=-1, keepdims=True) + EPS)
            vn = (vn * lng_ref[...] + lnb_ref[...]).astype(_BF16)
            for h in range(N_HEADS):
                sl = _head(h)
                s = jnp.dot(ws_ref[h], vn[:, sl], preferred_element_type=_F32) + bs_ref[:, sl]
                za = pa_s[rs, 2 * D_MODEL + h * HEAD_DIM:2 * D_MODEL + (h + 1) * HEAD_DIM]
                mix_s[out_rows, sl] = ((pa_s[rs, sl] * (za * _sigmoid(za))) * s).astype(_BF16)
        zb = jnp.dot(hb, wz_ref[...], preferred_element_type=_F32)
        z_s[rows, :] = (zb * _sigmoid(zb)) * gn_ref[...]
    _accumulate_range(bufs, range_ref)
    _recurrence(general, list(range(n_chunks)), p_s, q_ref, v_ref, st_ref, lb, tri2, mask, chunk - 1,
                bufs, emit)

    out = jnp.dot(mix_s[...], wo_ref[...], preferred_element_type=_F32)
    y_ref[0] = _rmsnorm_rows(x_ref[0] + out, fg_ref[...])


def _const_spec(shape):
    return pl.BlockSpec(shape, lambda b, j: (0,) * len(shape), pipeline_mode=pl.Buffered(1))


def _w_in_spec(group, n_groups=1):
    assert group % n_groups == 0
    return pl.BlockSpec((D_MODEL, n_groups * D_MODEL), lambda b, j: (0, group // n_groups),
                        pipeline_mode=pl.Buffered(1))


def _direction_constants(reverse, chunk):
    r, c = np.indices((chunk, chunk))
    mask = ((c >= r) if reverse else (c <= r)).astype(np.float32)
    tri2 = np.concatenate([mask, mask], axis=1)
    return jnp.asarray(tri2, _BF16), jnp.asarray(mask, _F32)


def _recurrence_scratch(tile, chunk):
    n_hc = (tile // chunk) * N_HEADS
    act_bf16 = pltpu.VMEM((tile, D_MODEL), _BF16)
    return [
        pltpu.VMEM((tile, D_MODEL), _F32),
        act_bf16, act_bf16, act_bf16,
        pltpu.VMEM((tile, D_MODEL), _F32),
        pltpu.VMEM((tile // chunk, D_MODEL), _F32),
        pltpu.VMEM((n_hc, chunk, HEAD_DIM + chunk), _BF16),
        pltpu.VMEM((n_hc, HEAD_DIM + chunk, HEAD_DIM), _BF16),
    ]


def _trunks(xs, p, tile=TILE, tile_bwd=TILE_BWD):
    def attempt(chunk, general=False):
        ys, lows = zip(*(_sweeps(x, p, tile, tile_bwd, chunk, general) for x in xs))
        return ys, functools.reduce(jnp.minimum, lows) >= _HALF_DECAY_FLOOR

    def narrow(_):
        ys, in_range = attempt(CHUNK_NARROW)
        return lax.cond(in_range, lambda ys: ys, lambda _: attempt(CHUNK_NARROW, general=True)[0], ys)

    ys, in_range = attempt(CHUNK)
    return lax.cond(in_range, lambda ys: ys, narrow, ys)


def _sweeps(x, p, tile, tile_bwd, chunk, general):
    bsz, seq, d = x.shape
    assert d == D_MODEL and seq % tile == 0 and seq % tile_bwd == 0 and tile % GMLP_CHUNK == 0
    cparams = pltpu.CompilerParams(dimension_semantics=("arbitrary", "arbitrary"),
                                   vmem_limit_bytes=VMEM_LIMIT_BYTES)
    state = pltpu.VMEM((N_HEADS, HEAD_DIM, HEAD_DIM), _F32)
    o_bwd, q_act, v_act, range_b = _bwd_sweep(general, x, p, tile_bwd, chunk, cparams, state)
    y, range_f = _main_sweep(general, x, o_bwd, q_act, v_act, p, tile, chunk, cparams, state)
    return y, jnp.min(jnp.minimum(range_b, range_f))


def _range_out():
    return (pl.BlockSpec((8, HEAD_DIM), lambda b, j: (0, 0)),
            jax.ShapeDtypeStruct((8, HEAD_DIM), _F32))


def _bwd_sweep(general, x, p, tile, chunk, cparams, state):
    bsz, seq, _ = x.shape
    nt = seq // tile
    tri2_b, mask_b = _direction_constants(True, chunk)
    rev_tile = lambda b, j: (b, nt - 1 - j, 0)
    act = lambda dtype: jax.ShapeDtypeStruct((bsz, seq, D_MODEL), dtype)
    range_spec, range_shape = _range_out()
    return pl.pallas_call(
        functools.partial(_bwd_kernel, general),
        grid=(bsz, nt),
        in_specs=[
            pl.BlockSpec((1, tile, D_MODEL), rev_tile),
            _const_spec((1, D_MODEL)),
            _w_in_spec(3), _w_in_spec(5), _w_in_spec(6),
            _const_spec((2, D_MODEL)),
            _const_spec((chunk, 2 * chunk)),
            _const_spec((chunk, chunk)),
        ],
        out_specs=[pl.BlockSpec((1, tile, D_MODEL), rev_tile)] * 3 + [range_spec],
        out_shape=[act(_BF16), act(_BF16), act(_BF16), range_shape],
        scratch_shapes=_recurrence_scratch(tile, chunk) + [state],
        compiler_params=cparams,
        name=f"hgrn_bwd_sweep_c{chunk}" + ("_general" if general else ""),
    )(x, p["norm_g"], p["w_in"], p["w_in"], p["w_in"], p["lb_bwd"], tri2_b, mask_b)


def _main_sweep(general, x, o_bwd, q_act, v_act, p, tile, chunk, cparams, state):
    bsz, seq, _ = x.shape
    nt = seq // tile
    tri2_f, mask_f = _direction_constants(False, chunk)
    fwd_tile = lambda b, j: (b, j, 0)
    range_spec, range_shape = _range_out()
    return pl.pallas_call(
        functools.partial(_main_kernel, general),
        grid=(bsz, nt),
        in_specs=[
            pl.BlockSpec((1, tile, D_MODEL), fwd_tile),
            pl.BlockSpec((1, tile, D_MODEL), fwd_tile),
            pl.BlockSpec((1, tile, D_MODEL), fwd_tile),
            pl.BlockSpec((1, tile, D_MODEL), fwd_tile),
            _const_spec((1, D_MODEL)),
            _w_in_spec(0, 3), _w_in_spec(4), _w_in_spec(7),
            _const_spec((1, D_MODEL)),
            _const_spec((1, D_MODEL)),
            _const_spec((N_HEADS, GMLP_CHUNK, GMLP_CHUNK)),
            _const_spec((GMLP_CHUNK, D_MODEL)),
            _const_spec((2, D_MODEL)),
            _const_spec((1, D_MODEL)),
            _const_spec((2 * D_MODEL, D_MODEL)),
            _const_spec((1, D_MODEL)),
            _const_spec((chunk, 2 * chunk)),
            _const_spec((chunk, chunk)),
        ],
        out_specs=[pl.BlockSpec((1, tile, D_MODEL), fwd_tile), range_spec],
        out_shape=[jax.ShapeDtypeStruct((bsz, seq, D_MODEL), _F32), range_shape],
        scratch_shapes=_recurrence_scratch(tile, chunk) + [
            pltpu.VMEM((tile, D_MODEL), _F32),
            pltpu.VMEM((tile, 2 * D_MODEL), _BF16),
            state,
            pltpu.VMEM((PROJ_ROWS, 3 * D_MODEL), _F32)],
        compiler_params=cparams,
        name=f"encoder_main_sweep_c{chunk}" + ("_general" if general else ""),
    )(x, o_bwd, q_act, v_act, p["norm_g"], p["w_in"], p["w_in"], p["w_in"], p["ln_g"], p["ln_b"], p["w_s"], p["b_s"],
      p["lb_fwd"], p["gn_g"], p["w_out"], p["final_g"], tri2_f, mask_f)


def _prepare(norm_g, w_in, ln_v_g, ln_v_b, w_s, b_s, lb_params, gn_g, w_out, final_g):
    row = lambda a: a.reshape(1, -1).astype(_F32)
    return {
        "norm_g": row(norm_g[0]),
        "w_in": w_in[0].astype(_BF16),
        "ln_g": row(ln_v_g[0]),
        "ln_b": row(ln_v_b[0]),
        "w_s": w_s[0].astype(_BF16),
        "b_s": jnp.repeat(b_s[0].T.astype(_F32), HEAD_DIM, axis=1),
        "lb_fwd": lb_params[0, :, :].astype(_F32),
        "lb_bwd": lb_params[1, :, :].astype(_F32),
        "gn_g": row(gn_g[0]),
        "w_out": w_out[0].astype(_BF16),
        "final_g": row(final_g),
    }


def kernel(x_prompt, x_sample, norm_g, w_in, ln_v_g, ln_v_b, w_s, b_s, lb_params, gn_g, w_out, final_g):
    p = _prepare(norm_g, w_in, ln_v_g, ln_v_b, w_s, b_s, lb_params, gn_g, w_out, final_g)
    return _trunks((x_prompt, x_sample), p)
```

```python
import dataclasses
import functools

import jax
import jax.numpy as jnp
import numpy as np
from jax import lax
from jax.experimental import pallas as pl
from jax.experimental.pallas import tpu as pltpu

D_MODEL = 1024
N_HEADS = 8
HEAD_DIM = 128
GMLP_CHUNK = 128
CHUNK = 128
CHUNK_NARROW = 64
EPS = 1e-6
TILE = 512
TILE_BWD = 1024
PROJ_ROWS = 256
VMEM_LIMIT_BYTES = 56 * 1024 * 1024

_BF16 = jnp.bfloat16
_F32 = jnp.float32
_LOG2E = 1.4426950408889634
_HALF_DECAY_FLOOR = 1e-26
_NT = (((1,), (1,)), ((), ()))
_TN = (((0,), (0,)), ((), ()))


def _head(h):
    return slice(h * HEAD_DIM, (h + 1) * HEAD_DIM)


def _exp(x, sign=1.0):
    return jnp.exp2(x * (sign * _LOG2E))


def _sigmoid(x):
    return 1.0 / (1.0 + _exp(x, -1.0))


def _rmsnorm_rows(x, gain):
    ms = jnp.mean(x * x, axis=-1, keepdims=True)
    return (x * lax.rsqrt(ms + EPS)) * gain


def _lower_bound(lb_pair):
    m = jnp.max(lb_pair, axis=0, keepdims=True)
    e = jnp.exp(lb_pair - m)
    return e[0:1, :] / jnp.sum(e, axis=0, keepdims=True)


def _cumsum_rows(g, tri2):
    hi = g.astype(_BF16)
    lo = (g - hi.astype(_F32)).astype(_BF16)
    g2 = jnp.concatenate([hi, lo], axis=0)
    return jnp.dot(tri2, g2, preferred_element_type=_F32)


def _split_chunk(c, f_ref, q_ref, lb, tri2, end_row, bufs):
    qt_s, kt_s, ks_s, _, eh_s, qa_s, _ = bufs
    chunk = qa_s.shape[1]
    rows = slice(c * chunk, (c + 1) * chunk)
    fg = lb + (1.0 - lb) * _sigmoid(f_ref[rows, :])
    g = jnp.log(fg)
    k = 1.0 - fg
    b = _cumsum_rows(g, tri2)
    half = 0.5 * b[end_row:end_row + 1, :]
    d = b - half
    e_half = _exp(half)
    kt = k * _exp(d, -1.0)
    kt_s[rows, :] = kt.astype(_BF16)
    ks_s[rows, :] = (kt * e_half).astype(_BF16)
    eh_s[c:c + 1, :] = e_half
    qt = q_ref[0, rows, :].astype(_F32) * _exp(d)
    qt_s[rows, :] = qt.astype(_BF16)
    qi = (qt * e_half).astype(_BF16)
    for h in range(N_HEADS):
        qa_s[c * N_HEADS + h, :, 0:HEAD_DIM] = qi[:, _head(h)]


def _general_chunk(c, f_ref, q_ref, lb, tri2, mask, end_row, bufs):
    _, _, ks_s, ed_s, _, qa_s, _ = bufs
    chunk = qa_s.shape[1]
    r0 = c * chunk
    rows = slice(r0, r0 + chunk)
    fg = lb + (1.0 - lb) * _sigmoid(f_ref[rows, :])
    g = jnp.log(fg)
    k = 1.0 - fg
    b = _cumsum_rows(g, tri2)
    total = b[end_row:end_row + 1, :]
    q = q_ref[0, rows, :].astype(_F32)
    ks_s[rows, :] = (k * _exp(total - b)).astype(_BF16)
    qi = (q * _exp(b)).astype(_BF16)
    for h in range(N_HEADS):
        qa_s[c * N_HEADS + h, :, 0:HEAD_DIM] = qi[:, _head(h)]
    ed_s[rows, :] = b
    f_ref[rows, :] = k
    col = lax.broadcasted_iota(jnp.int32, (chunk, chunk), 1)

    def one_key_row(s, acc):
        b_s = ed_s[pl.ds(r0 + s, 1), :]
        k_s = f_ref[pl.ds(r0 + s, 1), :]
        w = q * _exp(jnp.minimum(b - b_s, 0.0)) * k_s
        return tuple(a + jnp.where(col == s, jnp.sum(w[:, _head(h)], axis=-1, keepdims=True), 0.0)
                     for h, a in enumerate(acc))

    zeros = tuple(jnp.zeros((chunk, chunk), _F32) for _ in range(N_HEADS))
    acc = lax.fori_loop(0, chunk, one_key_row, zeros)
    for h in range(N_HEADS):
        qa_s[c * N_HEADS + h, :, HEAD_DIM:HEAD_DIM + chunk] = jnp.where(mask, acc[h], 0.0).astype(_BF16)


def _accumulate_range(bufs, range_ref):
    eh_s = bufs[4]
    low = jnp.min(eh_s[...], axis=0, keepdims=True)
    low = functools.reduce(jnp.minimum, [low[:, _head(h)] for h in range(N_HEADS)])
    range_ref[...] = jnp.minimum(range_ref[...], low)


def _recurrence(general, chunk_order, f_ref, q_ref, v_ref, st_ref, lb, tri2, mask, end_row, bufs, emit):
    if general:
        for c in sorted(chunk_order):
            _general_chunk(c, f_ref, q_ref, lb, tri2, mask, end_row, bufs)
    for c in chunk_order:
        _state_step(c, bufs, v_ref, st_ref, split_scores_mask=None if general else mask)
    for c in sorted(chunk_order):
        for h in range(N_HEADS):
            emit(c, h, _chunk_output(c, h, bufs))


def _state_step(c, bufs, v_ref, st_ref, split_scores_mask=None):
    qt_s, kt_s, ks_s, _, eh_s, qa_s, sv_s = bufs
    chunk = qa_s.shape[1]
    rows = slice(c * chunk, (c + 1) * chunk)
    e_half = eh_s[c:c + 1, :]
    decay = e_half * e_half
    for h in range(N_HEADS):
        sl = _head(h)
        i = c * N_HEADS + h
        if split_scores_mask is not None:
            scores = lax.dot_general(qt_s[rows, sl], kt_s[rows, sl], _NT, preferred_element_type=_F32)
            qa_s[i, :, HEAD_DIM:HEAD_DIM + chunk] = jnp.where(split_scores_mask, scores, 0.0).astype(_BF16)
        v = v_ref[0, rows, sl]
        upd = lax.dot_general(v, ks_s[rows, sl], _TN, preferred_element_type=_F32)
        st = st_ref[h]
        sv_s[i, 0:HEAD_DIM, :] = st.T.astype(_BF16)
        sv_s[i, HEAD_DIM:HEAD_DIM + chunk, :] = v
        st_ref[h] = st * decay[:, sl] + upd


def _chunk_output(c, h, bufs):
    qa_s, sv_s = bufs[-2:]
    i = c * N_HEADS + h
    return jnp.dot(qa_s[i], sv_s[i], preferred_element_type=_F32)


def _start_step(st_ref, range_ref):
    @pl.when(pl.program_id(1) == 0)
    def _():
        st_ref[...] = jnp.zeros_like(st_ref)

    @pl.when((pl.program_id(0) == 0) & (pl.program_id(1) == 0))
    def _():
        range_ref[...] = jnp.full(range_ref.shape, jnp.inf, range_ref.dtype)


def _normed_rows(x_ref, rows, ng_ref):
    return _rmsnorm_rows(x_ref[0, rows, :], ng_ref[...]).astype(_BF16)


def _bwd_kernel(general, x_ref, ng_ref, wq_ref, wf_ref, wi_ref, lb_ref, tri2_ref, mask_ref,
                o_ref, q_ref, v_ref, range_ref,
                p_s, qt_s, kt_s, ks_s, ed_s, eh_s, qa_s, sv_s, st_ref):
    chunk = qa_s.shape[1]
    n_chunks = x_ref.shape[1] // chunk
    _start_step(st_ref, range_ref)
    lb = _lower_bound(lb_ref[...])
    tri2 = tri2_ref[...]
    mask = mask_ref[...] > 0.5
    bufs = (qt_s, kt_s, ks_s, ed_s, eh_s, qa_s, sv_s)

    def emit(c, h, o):
        o_ref[0, c * chunk:(c + 1) * chunk, _head(h)] = o.astype(_BF16)

    n_blocks = x_ref.shape[1] // PROJ_ROWS
    group = max(n_blocks // 2, 1)
    for g in reversed(range(n_blocks // group)):
        for r in reversed(range(g * group, (g + 1) * group)):
            rows = slice(r * PROJ_ROWS, (r + 1) * PROJ_ROWS)
            hb = _normed_rows(x_ref, rows, ng_ref)
            p_s[rows, :] = jnp.dot(hb, wf_ref[...], preferred_element_type=_F32)
            qp = jnp.dot(hb, wq_ref[...], preferred_element_type=_F32)
            q_ref[0, rows, :] = (qp * _sigmoid(qp)).astype(_BF16)
            v_ref[0, rows, :] = jnp.dot(hb, wi_ref[...], preferred_element_type=_F32).astype(_BF16)
            for c in range(r * PROJ_ROWS // chunk, (r + 1) * PROJ_ROWS // chunk):
                _split_chunk(c, p_s, q_ref, lb, tri2, 0, bufs)
        chunks = range(g * group * PROJ_ROWS // chunk, (g + 1) * group * PROJ_ROWS // chunk)
        _recurrence(general, list(reversed(chunks)), p_s, q_ref, v_ref, st_ref, lb, tri2, mask, 0,
                    bufs, emit)
    _accumulate_range(bufs, range_ref)


def _main_kernel(general, x_ref, ob_ref, q_ref, v_ref, ng_ref, wa_ref, wf_ref, wz_ref, lng_ref, lnb_ref,
                 ws_ref, bs_ref, lb_ref, gn_ref, wo_ref, fg_ref, tri2_ref, mask_ref, y_ref, range_ref,
                 p_s, qt_s, kt_s, ks_s, ed_s, eh_s, qa_s, sv_s, z_s, mix_s, st_ref, pa_s):
    tile = x_ref.shape[1]
    chunk = qa_s.shape[1]
    n_chunks = tile // chunk
    _start_step(st_ref, range_ref)
    lb = _lower_bound(lb_ref[...])
    tri2 = tri2_ref[...]
    mask = mask_ref[...] > 0.5
    bufs = (qt_s, kt_s, ks_s, ed_s, eh_s, qa_s, sv_s)

    def emit(c, h, o):
        rows, sl = slice(c * chunk, (c + 1) * chunk), _head(h)
        o = o + ob_ref[0, rows, sl].astype(_F32)
        o = o * lax.rsqrt(jnp.mean(o * o, axis=-1, keepdims=True) + EPS)
        mix_s[rows, D_MODEL + h * HEAD_DIM:D_MODEL + (h + 1) * HEAD_DIM] = (
            o * z_s[rows, sl]).astype(_BF16)

    for r in range(tile // PROJ_ROWS):
        r0 = r * PROJ_ROWS
        rows = slice(r0, r0 + PROJ_ROWS)
        hb = _normed_rows(x_ref, rows, ng_ref)
        p_s[rows, :] = jnp.dot(hb, wf_ref[...], preferred_element_type=_F32)
        pa_s[...] = jnp.dot(hb, wa_ref[...], preferred_element_type=_F32)
        for c in range(r0 // chunk, (r0 + PROJ_ROWS) // chunk):
            _split_chunk(c, p_s, q_ref, lb, tri2, chunk - 1, bufs)
        for n in range(PROJ_ROWS // GMLP_CHUNK):
            rs = slice(n * GMLP_CHUNK, (n + 1) * GMLP_CHUNK)
            out_rows = slice(r0 + n * GMLP_CHUNK, r0 + (n + 1) * GMLP_CHUNK)
            vv = pa_s[rs, D_MODEL:2 * D_MODEL]
            vc = vv - jnp.mean(vv, axis=-1, keepdims=True)
            vn = vc * lax.rsqrt(jnp.mean(vc * vc, axis=-1, keepdims=True) + EPS)
            vn = (vn * lng_ref[...] + lnb_ref[...]).astype(_BF16)
            for h in range(N_HEADS):
                sl = _head(h)
                s = jnp.dot(ws_ref[h], vn[:, sl], preferred_element_type=_F32) + bs_ref[:, sl]
                za = pa_s[rs, 2 * D_MODEL + h * HEAD_DIM:2 * D_MODEL + (h + 1) * HEAD_DIM]
                mix_s[out_rows, sl] = ((pa_s[rs, sl] * (za * _sigmoid(za))) * s).astype(_BF16)
        zb = jnp.dot(hb, wz_ref[...], preferred_element_type=_F32)
        z_s[rows, :] = (zb * _sigmoid(zb)) * gn_ref[...]
    _accumulate_range(bufs, range_ref)
    _recurrence(general, list(range(n_chunks)), p_s, q_ref, v_ref, st_ref, lb, tri2, mask, chunk - 1,
                bufs, emit)

    out = jnp.dot(mix_s[...], wo_ref[...], preferred_element_type=_F32)
    y_ref[0] = _rmsnorm_rows(x_ref[0] + out, fg_ref[...])


def _const_spec(shape):
    return pl.BlockSpec(shape, lambda b, j: (0,) * len(shape), pipeline_mode=pl.Buffered(1))


def _w_in_spec(group, n_groups=1):
    assert group % n_groups == 0
    return pl.BlockSpec((D_MODEL, n_groups * D_MODEL), lambda b, j: (0, group // n_groups),
                        pipeline_mode=pl.Buffered(1))


def _direction_constants(reverse, chunk):
    r, c = np.indices((chunk, chunk))
    mask = ((c >= r) if reverse else (c <= r)).astype(np.float32)
    tri2 = np.concatenate([mask, mask], axis=1)
    return jnp.asarray(tri2, _BF16), jnp.asarray(mask, _F32)


def _recurrence_scratch(tile, chunk):
    n_hc = (tile // chunk) * N_HEADS
    act_bf16 = pltpu.VMEM((tile, D_MODEL), _BF16)
    return [
        pltpu.VMEM((tile, D_MODEL), _F32),
        act_bf16, act_bf16, act_bf16,
        pltpu.VMEM((tile, D_MODEL), _F32),
        pltpu.VMEM((tile // chunk, D_MODEL), _F32),
        pltpu.VMEM((n_hc, chunk, HEAD_DIM + chunk), _BF16),
        pltpu.VMEM((n_hc, HEAD_DIM + chunk, HEAD_DIM), _BF16),
    ]


def _trunks(xs, p, tile=TILE, tile_bwd=TILE_BWD):
    def attempt(chunk, general=False):
        ys, lows = zip(*(_sweeps(x, p, tile, tile_bwd, chunk, general) for x in xs))
        return ys, functools.reduce(jnp.minimum, lows) >= _HALF_DECAY_FLOOR

    def narrow(_):
        ys, in_range = attempt(CHUNK_NARROW)
        return lax.cond(in_range, lambda ys: ys, lambda _: attempt(CHUNK_NARROW, general=True)[0], ys)

    ys, in_range = attempt(CHUNK)
    return lax.cond(in_range, lambda ys: ys, narrow, ys)


def _sweeps(x, p, tile, tile_bwd, chunk, general):
    bsz, seq, d = x.shape
    assert d == D_MODEL and seq % tile == 0 and seq % tile_bwd == 0 and tile % GMLP_CHUNK == 0
    cparams = pltpu.CompilerParams(dimension_semantics=("arbitrary", "arbitrary"),
                                   vmem_limit_bytes=VMEM_LIMIT_BYTES)
    state = pltpu.VMEM((N_HEADS, HEAD_DIM, HEAD_DIM), _F32)
    o_bwd, q_act, v_act, range_b = _bwd_sweep(general, x, p, tile_bwd, chunk, cparams, state)
    y, range_f = _main_sweep(general, x, o_bwd, q_act, v_act, p, tile, chunk, cparams, state)
    return y, jnp.min(jnp.minimum(range_b, range_f))


def _range_out():
    return (pl.BlockSpec((8, HEAD_DIM), lambda b, j: (0, 0)),
            jax.ShapeDtypeStruct((8, HEAD_DIM), _F32))


def _bwd_sweep(general, x, p, tile, chunk, cparams, state):
    bsz, seq, _ = x.shape
    nt = seq // tile
    tri2_b, mask_b = _direction_constants(True, chunk)
    rev_tile = lambda b, j: (b, nt - 1 - j, 0)
    act = lambda dtype: jax.ShapeDtypeStruct((bsz, seq, D_MODEL), dtype)
    range_spec, range_shape = _range_out()
    return pl.pallas_call(
        functools.partial(_bwd_kernel, general),
        grid=(bsz, nt),
        in_specs=[
            pl.BlockSpec((1, tile, D_MODEL), rev_tile),
            _const_spec((1, D_MODEL)),
            _w_in_spec(3), _w_in_spec(5), _w_in_spec(6),
            _const_spec((2, D_MODEL)),
            _const_spec((chunk, 2 * chunk)),
            _const_spec((chunk, chunk)),
        ],
        out_specs=[pl.BlockSpec((1, tile, D_MODEL), rev_tile)] * 3 + [range_spec],
        out_shape=[act(_BF16), act(_BF16), act(_BF16), range_shape],
        scratch_shapes=_recurrence_scratch(tile, chunk) + [state],
        compiler_params=dataclasses.replace(cparams, allow_input_fusion=(False, False, True, True, True,
                                                                         False, False, False)),
        name=f"hgrn_bwd_sweep_c{chunk}" + ("_general" if general else ""),
    )(x, p["norm_g"], p["w_in"], p["w_in"], p["w_in"], p["lb_bwd"], tri2_b, mask_b)


def _main_sweep(general, x, o_bwd, q_act, v_act, p, tile, chunk, cparams, state):
    bsz, seq, _ = x.shape
    nt = seq // tile
    tri2_f, mask_f = _direction_constants(False, chunk)
    fwd_tile = lambda b, j: (b, j, 0)
    range_spec, range_shape = _range_out()
    return pl.pallas_call(
        functools.partial(_main_kernel, general),
        grid=(bsz, nt),
        in_specs=[
            pl.BlockSpec((1, tile, D_MODEL), fwd_tile),
            pl.BlockSpec((1, tile, D_MODEL), fwd_tile),
            pl.BlockSpec((1, tile, D_MODEL), fwd_tile),
            pl.BlockSpec((1, tile, D_MODEL), fwd_tile),
            _const_spec((1, D_MODEL)),
            _w_in_spec(0, 3), _w_in_spec(4), _w_in_spec(7),
            _const_spec((1, D_MODEL)),
            _const_spec((1, D_MODEL)),
            _const_spec((N_HEADS, GMLP_CHUNK, GMLP_CHUNK)),
            _const_spec((GMLP_CHUNK, D_MODEL)),
            _const_spec((2, D_MODEL)),
            _const_spec((1, D_MODEL)),
            _const_spec((2 * D_MODEL, D_MODEL)),
            _const_spec((1, D_MODEL)),
            _const_spec((chunk, 2 * chunk)),
            _const_spec((chunk, chunk)),
        ],
        out_specs=[pl.BlockSpec((1, tile, D_MODEL), fwd_tile), range_spec],
        out_shape=[jax.ShapeDtypeStruct((bsz, seq, D_MODEL), _F32), range_shape],
        scratch_shapes=_recurrence_scratch(tile, chunk) + [
            pltpu.VMEM((tile, D_MODEL), _F32),
            pltpu.VMEM((tile, 2 * D_MODEL), _BF16),
            state,
            pltpu.VMEM((PROJ_ROWS, 3 * D_MODEL), _F32)],
        compiler_params=dataclasses.replace(cparams, allow_input_fusion=tuple(
            i in (5, 6, 7, 10, 14) for i in range(18))),
        name=f"encoder_main_sweep_c{chunk}" + ("_general" if general else ""),
    )(x, o_bwd, q_act, v_act, p["norm_g"], p["w_in"], p["w_in"], p["w_in"], p["ln_g"], p["ln_b"], p["w_s"], p["b_s"],
      p["lb_fwd"], p["gn_g"], p["w_out"], p["final_g"], tri2_f, mask_f)


def _prepare(norm_g, w_in, ln_v_g, ln_v_b, w_s, b_s, lb_params, gn_g, w_out, final_g):
    row = lambda a: a.reshape(1, -1).astype(_F32)
    return {
        "norm_g": row(norm_g[0]),
        "w_in": w_in[0].astype(_BF16),
        "ln_g": row(ln_v_g[0]),
        "ln_b": row(ln_v_b[0]),
        "w_s": w_s[0].astype(_BF16),
        "b_s": jnp.repeat(b_s[0].T.astype(_F32), HEAD_DIM, axis=1),
        "lb_fwd": lb_params[0, :, :].astype(_F32),
        "lb_bwd": lb_params[1, :, :].astype(_F32),
        "gn_g": row(gn_g[0]),
        "w_out": w_out[0].astype(_BF16),
        "final_g": row(final_g),
    }


def kernel(x_prompt, x_sample, norm_g, w_in, ln_v_g, ln_v_b, w_s, b_s, lb_params, gn_g, w_out, final_g):
    p = _prepare(norm_g, w_in, ln_v_g, ln_v_b, w_s, b_s, lb_params, gn_g, w_out, final_g)
    return _trunks((x_prompt, x_sample), p)
```

```python
import functools

import jax
import jax.numpy as jnp
import numpy as np
from jax import lax
from jax.experimental import pallas as pl
from jax.experimental.pallas import tpu as pltpu

D_MODEL = 1024
N_HEADS = 8
HEAD_DIM = 128
GMLP_CHUNK = 128
CHUNK = 128
CHUNK_NARROW = 64
EPS = 1e-6
TILE = 512
TILE_BWD = 1024
PROJ_ROWS = 256
VMEM_LIMIT_BYTES = 56 * 1024 * 1024

_BF16 = jnp.bfloat16
_F32 = jnp.float32
_LOG2E = 1.4426950408889634
_HALF_DECAY_FLOOR = 1e-26
_NT = (((1,), (1,)), ((), ()))
_TN = (((0,), (0,)), ((), ()))


def _head(h):
    return slice(h * HEAD_DIM, (h + 1) * HEAD_DIM)


def _exp(x, sign=1.0):
    return jnp.exp2(x * (sign * _LOG2E))


def _sigmoid(x):
    return 1.0 / (1.0 + _exp(x, -1.0))


def _rmsnorm_rows(x, gain):
    ms = jnp.mean(x * x, axis=-1, keepdims=True)
    return (x * lax.rsqrt(ms + EPS)) * gain


def _lower_bound(lb_pair):
    m = jnp.max(lb_pair, axis=0, keepdims=True)
    e = jnp.exp(lb_pair - m)
    return e[0:1, :] / jnp.sum(e, axis=0, keepdims=True)


def _cumsum_rows(g, tri2):
    hi = g.astype(_BF16)
    lo = (g - hi.astype(_F32)).astype(_BF16)
    g2 = jnp.concatenate([hi, lo], axis=0)
    return jnp.dot(tri2, g2, preferred_element_type=_F32)


def _split_chunk(c, f_ref, q_ref, lb, tri2, end_row, bufs):
    qt_s, kt_s, ks_s, _, eh_s, qa_s, _ = bufs
    chunk = qa_s.shape[1]
    rows = slice(c * chunk, (c + 1) * chunk)
    fg = lb + (1.0 - lb) * _sigmoid(f_ref[rows, :])
    g = jnp.log(fg)
    k = 1.0 - fg
    b = _cumsum_rows(g, tri2)
    half = 0.5 * b[end_row:end_row + 1, :]
    d = b - half
    e_half = _exp(half)
    kt = k * _exp(d, -1.0)
    kt_s[rows, :] = kt.astype(_BF16)
    ks_s[rows, :] = (kt * e_half).astype(_BF16)
    eh_s[c:c + 1, :] = e_half
    qt = q_ref[0, rows, :].astype(_F32) * _exp(d)
    qt_s[rows, :] = qt.astype(_BF16)
    qi = (qt * e_half).astype(_BF16)
    for h in range(N_HEADS):
        qa_s[c * N_HEADS + h, :, 0:HEAD_DIM] = qi[:, _head(h)]


def _general_chunk(c, f_ref, q_ref, lb, tri2, mask, end_row, bufs):
    _, _, ks_s, ed_s, _, qa_s, _ = bufs
    chunk = qa_s.shape[1]
    r0 = c * chunk
    rows = slice(r0, r0 + chunk)
    fg = lb + (1.0 - lb) * _sigmoid(f_ref[rows, :])
    g = jnp.log(fg)
    k = 1.0 - fg
    b = _cumsum_rows(g, tri2)
    total = b[end_row:end_row + 1, :]
    q = q_ref[0, rows, :].astype(_F32)
    ks_s[rows, :] = (k * _exp(total - b)).astype(_BF16)
    qi = (q * _exp(b)).astype(_BF16)
    for h in range(N_HEADS):
        qa_s[c * N_HEADS + h, :, 0:HEAD_DIM] = qi[:, _head(h)]
    ed_s[rows, :] = b
    f_ref[rows, :] = k
    col = lax.broadcasted_iota(jnp.int32, (chunk, chunk), 1)

    def one_key_row(s, acc):
        b_s = ed_s[pl.ds(r0 + s, 1), :]
        k_s = f_ref[pl.ds(r0 + s, 1), :]
        w = q * _exp(jnp.minimum(b - b_s, 0.0)) * k_s
        return tuple(a + jnp.where(col == s, jnp.sum(w[:, _head(h)], axis=-1, keepdims=True), 0.0)
                     for h, a in enumerate(acc))

    zeros = tuple(jnp.zeros((chunk, chunk), _F32) for _ in range(N_HEADS))
    acc = lax.fori_loop(0, chunk, one_key_row, zeros)
    for h in range(N_HEADS):
        qa_s[c * N_HEADS + h, :, HEAD_DIM:HEAD_DIM + chunk] = jnp.where(mask, acc[h], 0.0).astype(_BF16)


def _accumulate_range(bufs, range_ref):
    eh_s = bufs[4]
    low = jnp.min(eh_s[...], axis=0, keepdims=True)
    low = functools.reduce(jnp.minimum, [low[:, _head(h)] for h in range(N_HEADS)])
    range_ref[...] = jnp.minimum(range_ref[...], low)


def _recurrence(general, chunk_order, f_ref, q_ref, v_ref, st_ref, lb, tri2, mask, end_row, bufs, emit):
    if general:
        for c in sorted(chunk_order):
            _general_chunk(c, f_ref, q_ref, lb, tri2, mask, end_row, bufs)
    for c in chunk_order:
        _state_step(c, bufs, v_ref, st_ref, split_scores_mask=None if general else mask)
    for c in sorted(chunk_order):
        for h in range(N_HEADS):
            emit(c, h, _chunk_output(c, h, bufs))


def _state_step(c, bufs, v_ref, st_ref, split_scores_mask=None):
    qt_s, kt_s, ks_s, _, eh_s, qa_s, sv_s = bufs
    chunk = qa_s.shape[1]
    rows = slice(c * chunk, (c + 1) * chunk)
    e_half = eh_s[c:c + 1, :]
    decay = e_half * e_half
    for h in range(N_HEADS):
        sl = _head(h)
        i = c * N_HEADS + h
        if split_scores_mask is not None:
            scores = lax.dot_general(qt_s[rows, sl], kt_s[rows, sl], _NT, preferred_element_type=_F32)
            qa_s[i, :, HEAD_DIM:HEAD_DIM + chunk] = jnp.where(split_scores_mask, scores, 0.0).astype(_BF16)
        v = v_ref[0, rows, sl]
        upd = lax.dot_general(v, ks_s[rows, sl], _TN, preferred_element_type=_F32)
        st = st_ref[h]
        sv_s[i, 0:HEAD_DIM, :] = st.T.astype(_BF16)
        sv_s[i, HEAD_DIM:HEAD_DIM + chunk, :] = v
        st_ref[h] = st * decay[:, sl] + upd


def _chunk_output(c, h, bufs):
    qa_s, sv_s = bufs[-2:]
    i = c * N_HEADS + h
    return jnp.dot(qa_s[i], sv_s[i], preferred_element_type=_F32)


def _start_step(st_ref, range_ref):
    @pl.when(pl.program_id(1) == 0)
    def _():
        st_ref[...] = jnp.zeros_like(st_ref)

    @pl.when((pl.program_id(0) == 0) & (pl.program_id(1) == 0))
    def _():
        range_ref[...] = jnp.full(range_ref.shape, jnp.inf, range_ref.dtype)


def _normed_rows(x_ref, rows, ng_ref):
    return _rmsnorm_rows(x_ref[0, rows, :], ng_ref[...]).astype(_BF16)


def _bwd_kernel(general, x_ref, ng_ref, wq_ref, wf_ref, wi_ref, lb_ref, tri2_ref, mask_ref,
                o_ref, q_ref, v_ref, range_ref,
                p_s, qt_s, kt_s, ks_s, ed_s, eh_s, qa_s, sv_s, st_ref):
    chunk = qa_s.shape[1]
    n_chunks = x_ref.shape[1] // chunk
    _start_step(st_ref, range_ref)
    lb = _lower_bound(lb_ref[...])
    tri2 = tri2_ref[...]
    mask = mask_ref[...] > 0.5
    bufs = (qt_s, kt_s, ks_s, ed_s, eh_s, qa_s, sv_s)

    def emit(c, h, o):
        o_ref[0, c * chunk:(c + 1) * chunk, _head(h)] = o.astype(_BF16)

    n_blocks = x_ref.shape[1] // PROJ_ROWS
    group = max(n_blocks // 2, 1)
    for g in reversed(range(n_blocks // group)):
        for r in reversed(range(g * group, (g + 1) * group)):
            rows = slice(r * PROJ_ROWS, (r + 1) * PROJ_ROWS)
            hb = _normed_rows(x_ref, rows, ng_ref)
            p_s[rows, :] = jnp.dot(hb, wf_ref[...], preferred_element_type=_F32)
            qp = jnp.dot(hb, wq_ref[...], preferred_element_type=_F32)
            q_ref[0, rows, :] = (qp * _sigmoid(qp)).astype(_BF16)
            v_ref[0, rows, :] = jnp.dot(hb, wi_ref[...], preferred_element_type=_F32).astype(_BF16)
            for c in range(r * PROJ_ROWS // chunk, (r + 1) * PROJ_ROWS // chunk):
                _split_chunk(c, p_s, q_ref, lb, tri2, 0, bufs)
            if r == (g + 1) * group - 1 and g + 1 < n_blocks // group:
                later = range((g + 1) * group * PROJ_ROWS // chunk, (g + 2) * group * PROJ_ROWS // chunk)
                _recurrence(general, list(reversed(later)), p_s, q_ref, v_ref, st_ref, lb, tri2, mask, 0,
                            bufs, emit)
    first = range(0, group * PROJ_ROWS // chunk)
    _recurrence(general, list(reversed(first)), p_s, q_ref, v_ref, st_ref, lb, tri2, mask, 0, bufs, emit)
    _accumulate_range(bufs, range_ref)


def _main_kernel(general, x_ref, ob_ref, q_ref, v_ref, ng_ref, wa_ref, wf_ref, wz_ref, lng_ref, lnb_ref,
                 ws_ref, bs_ref, lb_ref, gn_ref, wo_ref, fg_ref, tri2_ref, mask_ref, y_ref, range_ref,
                 p_s, qt_s, kt_s, ks_s, ed_s, eh_s, qa_s, sv_s, z_s, mix_s, st_ref, pa_s):
    tile = x_ref.shape[1]
    chunk = qa_s.shape[1]
    n_chunks = tile // chunk
    _start_step(st_ref, range_ref)
    lb = _lower_bound(lb_ref[...])
    tri2 = tri2_ref[...]
    mask = mask_ref[...] > 0.5
    bufs = (qt_s, kt_s, ks_s, ed_s, eh_s, qa_s, sv_s)

    def emit(c, h, o):
        rows, sl = slice(c * chunk, (c + 1) * chunk), _head(h)
        o = o + ob_ref[0, rows, sl].astype(_F32)
        o = o * lax.rsqrt(jnp.mean(o * o, axis=-1, keepdims=True) + EPS)
        mix_s[rows, D_MODEL + h * HEAD_DIM:D_MODEL + (h + 1) * HEAD_DIM] = (
            o * z_s[rows, sl]).astype(_BF16)

    for r in range(tile // PROJ_ROWS):
        r0 = r * PROJ_ROWS
        rows = slice(r0, r0 + PROJ_ROWS)
        hb = _normed_rows(x_ref, rows, ng_ref)
        p_s[rows, :] = jnp.dot(hb, wf_ref[...], preferred_element_type=_F32)
        pa_s[...] = jnp.dot(hb, wa_ref[...], preferred_element_type=_F32)
        for c in range(r0 // chunk, (r0 + PROJ_ROWS) // chunk):
            _split_chunk(c, p_s, q_ref, lb, tri2, chunk - 1, bufs)
        n_blocks = PROJ_ROWS // GMLP_CHUNK
        vns = []
        for n in range(n_blocks):
            vv = pa_s[n * GMLP_CHUNK:(n + 1) * GMLP_CHUNK, D_MODEL:2 * D_MODEL]
            vc = vv - jnp.mean(vv, axis=-1, keepdims=True)
            vn = vc * lax.rsqrt(jnp.mean(vc * vc, axis=-1, keepdims=True) + EPS)
            vns.append((vn * lng_ref[...] + lnb_ref[...]).astype(_BF16))
        for h in range(N_HEADS):
            sl = _head(h)
            mixed = jnp.dot(ws_ref[h], jnp.concatenate([vn[:, sl] for vn in vns], axis=1),
                            preferred_element_type=_F32)
            for n in range(n_blocks):
                rs = slice(n * GMLP_CHUNK, (n + 1) * GMLP_CHUNK)
                s = mixed[:, n * HEAD_DIM:(n + 1) * HEAD_DIM] + bs_ref[:, sl]
                za = pa_s[rs, 2 * D_MODEL + h * HEAD_DIM:2 * D_MODEL + (h + 1) * HEAD_DIM]
                mix_s[r0 + n * GMLP_CHUNK:r0 + (n + 1) * GMLP_CHUNK, sl] = (
                    (pa_s[rs, sl] * (za * _sigmoid(za))) * s).astype(_BF16)
        zb = jnp.dot(hb, wz_ref[...], preferred_element_type=_F32)
        z_s[rows, :] = (zb * _sigmoid(zb)) * gn_ref[...]
    _accumulate_range(bufs, range_ref)
    _recurrence(general, list(range(n_chunks)), p_s, q_ref, v_ref, st_ref, lb, tri2, mask, chunk - 1,
                bufs, emit)

    out = jnp.dot(mix_s[...], wo_ref[...], preferred_element_type=_F32)
    y_ref[0] = _rmsnorm_rows(x_ref[0] + out, fg_ref[...])


def _const_spec(shape):
    return pl.BlockSpec(shape, lambda b, j: (0,) * len(shape), pipeline_mode=pl.Buffered(1))


def _w_in_spec(group, n_groups=1):
    assert group % n_groups == 0
    return pl.BlockSpec((D_MODEL, n_groups * D_MODEL), lambda b, j: (0, group // n_groups),
                        pipeline_mode=pl.Buffered(1))


def _direction_constants(reverse, chunk):
    r, c = np.indices((chunk, chunk))
    mask = ((c >= r) if reverse else (c <= r)).astype(np.float32)
    tri2 = np.concatenate([mask, mask], axis=1)
    return jnp.asarray(tri2, _BF16), jnp.asarray(mask, _F32)


def _recurrence_scratch(tile, chunk):
    n_hc = (tile // chunk) * N_HEADS
    act_bf16 = pltpu.VMEM((tile, D_MODEL), _BF16)
    return [
        pltpu.VMEM((tile, D_MODEL), _F32),
        act_bf16, act_bf16, act_bf16,
        pltpu.VMEM((tile, D_MODEL), _F32),
        pltpu.VMEM((tile // chunk, D_MODEL), _F32),
        pltpu.VMEM((n_hc, chunk, HEAD_DIM + chunk), _BF16),
        pltpu.VMEM((n_hc, HEAD_DIM + chunk, HEAD_DIM), _BF16),
    ]


def _trunks(xs, p, tile=TILE, tile_bwd=TILE_BWD):
    def attempt(chunk, general=False):
        ys, lows = zip(*(_sweeps(x, p, tile, tile_bwd, chunk, general) for x in xs))
        return ys, functools.reduce(jnp.minimum, lows) >= _HALF_DECAY_FLOOR

    def narrow(_):
        ys, in_range = attempt(CHUNK_NARROW)
        return lax.cond(in_range, lambda ys: ys, lambda _: attempt(CHUNK_NARROW, general=True)[0], ys)

    ys, in_range = attempt(CHUNK)
    return lax.cond(in_range, lambda ys: ys, narrow, ys)


def _sweeps(x, p, tile, tile_bwd, chunk, general):
    bsz, seq, d = x.shape
    assert d == D_MODEL and seq % tile == 0 and seq % tile_bwd == 0 and tile % GMLP_CHUNK == 0
    cparams = pltpu.CompilerParams(dimension_semantics=("arbitrary", "arbitrary"),
                                   vmem_limit_bytes=VMEM_LIMIT_BYTES)
    state = pltpu.VMEM((N_HEADS, HEAD_DIM, HEAD_DIM), _F32)
    o_bwd, q_act, v_act, range_b = _bwd_sweep(general, x, p, tile_bwd, chunk, cparams, state)
    y, range_f = _main_sweep(general, x, o_bwd, q_act, v_act, p, tile, chunk, cparams, state)
    return y, jnp.min(jnp.minimum(range_b, range_f))


def _range_out():
    return (pl.BlockSpec((8, HEAD_DIM), lambda b, j: (0, 0)),
            jax.ShapeDtypeStruct((8, HEAD_DIM), _F32))


def _bwd_sweep(general, x, p, tile, chunk, cparams, state):
    bsz, seq, _ = x.shape
    nt = seq // tile
    tri2_b, mask_b = _direction_constants(True, chunk)
    rev_tile = lambda b, j: (b, nt - 1 - j, 0)
    act = lambda dtype: jax.ShapeDtypeStruct((bsz, seq, D_MODEL), dtype)
    range_spec, range_shape = _range_out()
    return pl.pallas_call(
        functools.partial(_bwd_kernel, general),
        grid=(bsz, nt),
        in_specs=[
            pl.BlockSpec((1, tile, D_MODEL), rev_tile),
            _const_spec((1, D_MODEL)),
            _w_in_spec(3), _w_in_spec(5), _w_in_spec(6),
            _const_spec((2, D_MODEL)),
            _const_spec((chunk, 2 * chunk)),
            _const_spec((chunk, chunk)),
        ],
        out_specs=[pl.BlockSpec((1, tile, D_MODEL), rev_tile)] * 3 + [range_spec],
        out_shape=[act(_BF16), act(_BF16), act(_BF16), range_shape],
        scratch_shapes=_recurrence_scratch(tile, chunk) + [state],
        compiler_params=cparams,
        name=f"hgrn_bwd_sweep_c{chunk}" + ("_general" if general else ""),
    )(x, p["norm_g"], p["w_in"], p["w_in"], p["w_in"], p["lb_bwd"], tri2_b, mask_b)


def _main_sweep(general, x, o_bwd, q_act, v_act, p, tile, chunk, cparams, state):
    bsz, seq, _ = x.shape
    nt = seq // tile
    tri2_f, mask_f = _direction_constants(False, chunk)
    fwd_tile = lambda b, j: (b, j, 0)
    range_spec, range_shape = _range_out()
    return pl.pallas_call(
        functools.partial(_main_kernel, general),
        grid=(bsz, nt),
        in_specs=[
            pl.BlockSpec((1, tile, D_MODEL), fwd_tile),
            pl.BlockSpec((1, tile, D_MODEL), fwd_tile),
            pl.BlockSpec((1, tile, D_MODEL), fwd_tile),
            pl.BlockSpec((1, tile, D_MODEL), fwd_tile),
            _const_spec((1, D_MODEL)),
            _w_in_spec(0, 3), _w_in_spec(4), _w_in_spec(7),
            _const_spec((1, D_MODEL)),
            _const_spec((1, D_MODEL)),
            _const_spec((N_HEADS, GMLP_CHUNK, GMLP_CHUNK)),
            _const_spec((GMLP_CHUNK, D_MODEL)),
            _const_spec((2, D_MODEL)),
            _const_spec((1, D_MODEL)),
            _const_spec((2 * D_MODEL, D_MODEL)),
            _const_spec((1, D_MODEL)),
            _const_spec((chunk, 2 * chunk)),
            _const_spec((chunk, chunk)),
        ],
        out_specs=[pl.BlockSpec((1, tile, D_MODEL), fwd_tile), range_spec],
        out_shape=[jax.ShapeDtypeStruct((bsz, seq, D_MODEL), _F32), range_shape],
        scratch_shapes=_recurrence_scratch(tile, chunk) + [
            pltpu.VMEM((tile, D_MODEL), _F32),
            pltpu.VMEM((tile, 2 * D_MODEL), _BF16),
            state,
            pltpu.VMEM((PROJ_ROWS, 3 * D_MODEL), _F32)],
        compiler_params=cparams,
        name=f"encoder_main_sweep_c{chunk}" + ("_general" if general else ""),
    )(x, o_bwd, q_act, v_act, p["norm_g"], p["w_in"], p["w_in"], p["w_in"], p["ln_g"], p["ln_b"], p["w_s"], p["b_s"],
      p["lb_fwd"], p["gn_g"], p["w_out"], p["final_g"], tri2_f, mask_f)


def _prepare(norm_g, w_in, ln_v_g, ln_v_b, w_s, b_s, lb_params, gn_g, w_out, final_g):
    row = lambda a: a.reshape(1, -1).astype(_F32)
    return {
        "norm_g": row(norm_g[0]),
        "w_in": w_in[0].astype(_BF16),
        "ln_g": row(ln_v_g[0]),
        "ln_b": row(ln_v_b[0]),
        "w_s": w_s[0].astype(_BF16),
        "b_s": jnp.repeat(b_s[0].T.astype(_F32), HEAD_DIM, axis=1),
        "lb_fwd": lb_params[0, :, :].astype(_F32),
        "lb_bwd": lb_params[1, :, :].astype(_F32),
        "gn_g": row(gn_g[0]),
        "w_out": w_out[0].astype(_BF16),
        "final_g": row(final_g),
    }


def kernel(x_prompt, x_sample, norm_g, w_in, ln_v_g, ln_v_b, w_s, b_s, lb_params, gn_g, w_out, final_g):
    p = _prepare(norm_g, w_in, ln_v_g, ln_v_b, w_s, b_s, lb_params, gn_g, w_out, final_g)
    return _trunks((x_prompt, x_sample), p)
```
